```python
import jax
import jax.numpy as jnp
from jax import lax
import numpy as np

D_MODEL = 2048
BATCH = 2
SEQ = 8192
DEPTH = 4

BRANCH_WIDTH = D_MODEL // 2
HEAD_DIM = 128
A_HEADS = BRANCH_WIDTH // HEAD_DIM
C_HEADS = BRANCH_WIDTH // HEAD_DIM
SHORT_CONV = 4
CONF_CONV = 31
CHUNK = 64
Q_BLOCK = 128
N_BRANCH = 3
NORM_EPS = 1e-6
SPLIT_SIZES = (3 * BRANCH_WIDTH, BRANCH_WIDTH, A_HEADS, A_HEADS,
               2 * BRANCH_WIDTH, BRANCH_WIDTH,
               3 * BRANCH_WIDTH, BRANCH_WIDTH, C_HEADS,
               N_BRANCH * D_MODEL)
N_IN = sum(SPLIT_SIZES)

kernel_name = 'hybrid_gdn_conformer_fox_parallel'


def rms_norm(x, w):
    x32 = x.astype(jnp.float32)
    y = x32 * lax.rsqrt(jnp.mean(x32 * x32, axis=-1, keepdims=True) + NORM_EPS)
    return y.astype(x.dtype) * w


def layer_norm(x, w, b):
    x32 = x.astype(jnp.float32)
    xc = x32 - jnp.mean(x32, axis=-1, keepdims=True)
    y = xc * lax.rsqrt(jnp.mean(xc * xc, axis=-1, keepdims=True) + NORM_EPS)
    return y.astype(x.dtype) * w + b


def l2_normalize(x):
    x32 = x.astype(jnp.float32)
    return x32 * lax.rsqrt(jnp.sum(x32 * x32, axis=-1, keepdims=True) + NORM_EPS)


def causal_depthwise_conv(x, w):
    k_width, ch = w.shape
    return lax.conv_general_dilated(
        x, w[:, None, :].astype(x.dtype), window_strides=(1,), padding=[(k_width - 1, 0)],
        dimension_numbers=('NWC', 'WIO', 'NWC'), feature_group_count=ch)


def gated_delta_rule(q, k, v, log_a, beta):
    f32 = jnp.float32
    bsz, seq, nh, dk = q.shape
    dv = v.shape[-1]
    nc = seq // CHUNK
    qc = q.astype(f32).reshape(bsz, nc, CHUNK, nh, dk) * (dk ** -0.5)
    kc = k.astype(f32).reshape(bsz, nc, CHUNK, nh, dk)
    vc = v.astype(f32).reshape(bsz, nc, CHUNK, nh, dv)
    bc = beta.astype(f32).reshape(bsz, nc, CHUNK, nh)
    g = jnp.cumsum(log_a.astype(f32).reshape(bsz, nc, CHUNK, nh), axis=2)
    gh = jnp.transpose(g, (0, 1, 3, 2))
    bh = jnp.transpose(bc, (0, 1, 3, 2))
    incl = jnp.tril(jnp.ones((CHUNK, CHUNK), dtype=bool))
    strict = jnp.tril(jnp.ones((CHUNK, CHUNK), dtype=bool), -1)
    diff = gh[..., :, None] - gh[..., None, :]
    decay = jnp.exp(jnp.where(incl, diff, -jnp.inf))
    kk = jnp.einsum('bnihd,bnjhd->bnhij', kc, kc)
    lower = jnp.where(strict, bh[..., :, None] * kk * decay, 0.0)
    eye = jnp.eye(CHUNK, dtype=f32)
    t_inv = lax.linalg.triangular_solve(lower + eye, jnp.broadcast_to(eye, lower.shape),
                                        left_side=True, lower=True)
    u0 = jnp.einsum('bnhij,bnjhe->bnihe', t_inv, vc * bc[..., None])
    w_cum = jnp.einsum('bnhij,bnjhd->bnihd', t_inv, kc * (bc * jnp.exp(g))[..., None])
    qk = jnp.einsum('bnihd,bnjhd->bnhij', qc, kc) * decay
    q_dec = qc * jnp.exp(g)[..., None]
    g_last = g[:, :, -1:, :]
    k_tail = kc * jnp.exp(g_last - g)[..., None]
    chunk_decay = jnp.exp(g_last[:, :, 0, :])

    def step(state, inp):
        u0_n, w_n, qk_n, qd_n, kt_n, dec_n = inp
        u = u0_n - jnp.einsum('bihd,bhde->bihe', w_n, state)
        o = jnp.einsum('bihd,bhde->bihe', qd_n, state) + jnp.einsum('bhij,bjhe->bihe', qk_n, u)
        state = state * dec_n[:, :, None, None] + jnp.einsum('bihd,bihe->bhde', kt_n, u)
        return state, o

    xs = tuple(jnp.moveaxis(t, 1, 0) for t in (u0, w_cum, qk, q_dec, k_tail, chunk_decay))
    state0 = jnp.zeros((bsz, nh, dk, dv), f32)
    _, out = lax.scan(step, state0, xs)
    return jnp.moveaxis(out, 0, 1).reshape(bsz, seq, nh, dv).astype(v.dtype)


def forgetting_attention(q, k, v, log_f):
    bsz, seq, nh, hd = q.shape
    nb = seq // Q_BLOCK
    scale = hd ** -0.5
    c = jnp.cumsum(log_f, axis=1)
    c_key = jnp.transpose(c, (0, 2, 1))
    pos = jnp.arange(seq, dtype=jnp.int32)
    q_blocks = jnp.moveaxis(q.reshape(bsz, nb, Q_BLOCK, nh, hd), 1, 0)
    c_blocks = jnp.moveaxis(c.reshape(bsz, nb, Q_BLOCK, nh), 1, 0)
    q_pos = pos.reshape(nb, Q_BLOCK)

    def block(args):
        q_blk, c_blk, qp = args
        s = jnp.einsum('bqhd,bkhd->bhqk', q_blk, k).astype(jnp.float32) * scale
        s = s + jnp.transpose(c_blk, (0, 2, 1))[..., None] - c_key[:, :, None, :]
        s = jnp.where(qp[:, None] >= pos[None, :], s, -jnp.inf)
        p = jax.nn.softmax(s, axis=-1).astype(v.dtype)
        return jnp.einsum('bhqk,bkhd->bqhd', p, v)

    out = lax.map(block, (q_blocks, c_blocks, q_pos))
    return jnp.moveaxis(out, 0, 1).reshape(bsz, seq, nh, hd)


def hybrid_layer(x, pre_w, post_w, w_in, conv_qkv_w, a_log, dt_bias, o_norm_w,
                 conv_w, conv_b, ln_w, ln_b, f_bias, w_branch, w_out):
    f32 = jnp.float32
    bsz, seq, _ = x.shape
    h = rms_norm(x, pre_w)
    proj = jnp.einsum('bsd,dn->bsn', h, w_in)
    split_at = [int(i) for i in np.cumsum(SPLIT_SIZES)[:-1]]
    (qkv_a, z_a, beta_a, alpha_a, glu_in, z_b,
     qkv_c, z_c, f_logit, gate_logit) = jnp.split(proj, split_at, axis=-1)

    qkv_a = jax.nn.silu(causal_depthwise_conv(qkv_a, conv_qkv_w))
    qkv_a = qkv_a.reshape(bsz, seq, 3, A_HEADS, HEAD_DIM)
    beta = jax.nn.sigmoid(beta_a.astype(f32))
    log_a = -jnp.exp(a_log.astype(f32)) * jax.nn.softplus(alpha_a.astype(f32) + dt_bias.astype(f32))
    o_a = gated_delta_rule(l2_normalize(qkv_a[:, :, 0]), l2_normalize(qkv_a[:, :, 1]),
                           qkv_a[:, :, 2], log_a, beta)
    o_a = rms_norm(o_a, o_norm_w) * jax.nn.silu(z_a.reshape(bsz, seq, A_HEADS, HEAD_DIM))
    y_a = o_a.reshape(bsz, seq, BRANCH_WIDTH)

    val, gate = jnp.split(glu_in, 2, axis=-1)
    u = val * jax.nn.sigmoid(gate)
    u = causal_depthwise_conv(u, conv_w) + conv_b
    u = jax.nn.silu(layer_norm(u, ln_w, ln_b))
    y_b = u * jax.nn.silu(z_b)

    qkv_c = qkv_c.reshape(bsz, seq, 3, C_HEADS, HEAD_DIM)
    log_f = jax.nn.log_sigmoid(f_logit.astype(f32) + f_bias.astype(f32))
    o_c = forgetting_attention(qkv_c[:, :, 0], qkv_c[:, :, 1], qkv_c[:, :, 2], log_f)
    y_c = o_c.reshape(bsz, seq, BRANCH_WIDTH) * jax.nn.silu(z_c)

    branches = jnp.einsum('nbsw,nwd->nbsd', jnp.stack([y_a, y_b, y_c]), w_branch)
    gates = jax.nn.sigmoid(gate_logit.reshape(bsz, seq, N_BRANCH, D_MODEL))
    merged = jnp.einsum('bsnd,nbsd->bsd', gates, branches)
    out = jnp.einsum('bsd,de->bse', merged, w_out)
    return x + rms_norm(out, post_w)


def setup_inputs(seed: int = 0) -> dict:
    key = jax.random.key(seed)
    ks = jax.random.split(key, 16)
    f32 = jnp.float32

    def nrm(k, shape, scale):
        return scale * jax.random.normal(k, shape, f32)

    x = jax.random.normal(ks[0], (BATCH, SEQ, D_MODEL), f32)
    pre_norm_w = 1.0 + nrm(ks[1], (DEPTH, D_MODEL), 0.02)
    post_norm_w = 1.0 + nrm(ks[2], (DEPTH, D_MODEL), 0.02)
    w_in = nrm(ks[3], (DEPTH, D_MODEL, N_IN), D_MODEL ** -0.5)
    conv_qkv_w = nrm(ks[4], (DEPTH, SHORT_CONV, 3 * BRANCH_WIDTH), SHORT_CONV ** -0.5)
    a_log = jnp.log(jax.random.uniform(ks[5], (DEPTH, A_HEADS), f32, 1.0, 16.0))
    dt = jnp.exp(jax.random.uniform(ks[6], (DEPTH, A_HEADS), f32,
                                    float(np.log(1e-3)), float(np.log(1e-1))))
    dt_bias = dt + jnp.log(-jnp.expm1(-dt))
    o_norm_w = 1.0 + nrm(ks[7], (DEPTH, HEAD_DIM), 0.02)
    conv_w = nrm(ks[8], (DEPTH, CONF_CONV, BRANCH_WIDTH), CONF_CONV ** -0.5)
    conv_b = nrm(ks[9], (DEPTH, BRANCH_WIDTH), 0.01)
    ln_w = 1.0 + nrm(ks[10], (DEPTH, BRANCH_WIDTH), 0.02)
    ln_b = nrm(ks[11], (DEPTH, BRANCH_WIDTH), 0.01)
    f_bias = 3.0 + nrm(ks[12], (DEPTH, C_HEADS), 0.5)
    w_branch = nrm(ks[13], (DEPTH, N_BRANCH, BRANCH_WIDTH, D_MODEL), BRANCH_WIDTH ** -0.5)
    w_out = nrm(ks[14], (DEPTH, D_MODEL, D_MODEL), D_MODEL ** -0.5)
    return {'x': x, 'pre_norm_w': pre_norm_w, 'post_norm_w': post_norm_w, 'w_in': w_in,
            'conv_qkv_w': conv_qkv_w, 'a_log': a_log, 'dt_bias': dt_bias, 'o_norm_w': o_norm_w,
            'conv_w': conv_w, 'conv_b': conv_b, 'ln_w': ln_w, 'ln_b': ln_b, 'f_bias': f_bias,
            'w_branch': w_branch, 'w_out': w_out}


def reference(x, pre_norm_w, post_norm_w, w_in, conv_qkv_w, a_log, dt_bias, o_norm_w,
              conv_w, conv_b, ln_w, ln_b, f_bias, w_branch, w_out):
    for l in range(DEPTH):
        x = hybrid_layer(x, pre_norm_w[l], post_norm_w[l], w_in[l], conv_qkv_w[l], a_log[l],
                         dt_bias[l], o_norm_w[l], conv_w[l], conv_b[l], ln_w[l], ln_b[l],
                         f_bias[l], w_branch[l], w_out[l])
    return x
```

```python
import functools

import jax
import jax.numpy as jnp
from jax import lax
from jax.experimental import pallas as pl
from jax.experimental.pallas import tpu as pltpu

F32 = jnp.float32
BF16 = jnp.bfloat16

HEAD_DIM = 128
CHUNK = 64
SHORT_CONV = 4
CONF_CONV = 31
NORM_EPS = 1e-6
N_BRANCH = 3
N_SMALL = 128
N_SMALL_T = 32
V7X_VMEM_LIMIT = 48 * 1024 * 1024


def _cparams(n_axes):
    return pltpu.CompilerParams(dimension_semantics=("arbitrary",) * n_axes,
                                vmem_limit_bytes=V7X_VMEM_LIMIT)


def _sigmoid(x):
    return 1.0 / (1.0 + jnp.exp(-x))


def _silu(x):
    return x * _sigmoid(x)


def _softplus(x):
    return jnp.maximum(x, 0.0) + jnp.log(1.0 + jnp.exp(-jnp.abs(x)))


def _dot(a, b):
    return jnp.dot(a, b, preferred_element_type=F32)


def _dot_nt(a, b):
    return lax.dot_general(a, b, (((1,), (1,)), ((), ())), preferred_element_type=F32)


def _dot_tn(a, b):
    return lax.dot_general(a, b, (((0,), (0,)), ((), ())), preferred_element_type=F32)


def _dot_f32(a, b):
    return jnp.dot(a, b, preferred_element_type=F32, precision=lax.Precision.HIGHEST)


def _prenorm_kernel(x_ref, w_ref, h_ref):
    x = x_ref[...]
    y = x * lax.rsqrt(jnp.mean(x * x, axis=-1, keepdims=True) + NORM_EPS)
    h_ref[...] = (y * w_ref[...]).astype(BF16)


def _prenorm(x2, w, tm):
    m, d = x2.shape
    return pl.pallas_call(
        _prenorm_kernel,
        out_shape=jax.ShapeDtypeStruct((m, d), BF16),
        grid=(m // tm,),
        in_specs=[pl.BlockSpec((tm, d), lambda i: (i, 0)),
                  pl.BlockSpec((1, d), lambda i: (0, 0))],
        out_specs=pl.BlockSpec((tm, d), lambda i: (i, 0)),
        compiler_params=_cparams(1),
        name="prenorm",
    )(x2, w.reshape(1, d))


def _proj_kernel(h_ref, w_ref, o_ref, *, epilogue, q_scale):
    acc = _dot(h_ref[...], w_ref[...])
    if epilogue == "f32":
        o_ref[...] = acc
    elif epilogue == "silu_bf16":
        o_ref[...] = _silu(acc).astype(BF16)
    elif epilogue == "glu":
        half = acc.shape[1] // 2
        o_ref[...] = acc[:, :half] * _sigmoid(acc[:, half:])
    elif epilogue == "qkv_bf16":
        scale = jnp.where(pl.program_id(0) == 0, q_scale, 1.0).astype(F32)
        o_ref[...] = (acc * scale).astype(BF16)
    else:
        raise ValueError(epilogue)


def _proj(h, w, *, epilogue, tm, tn, out_dtype, q_scale=1.0):
    m, k = h.shape
    n = w.shape[1]
    tn_out = tn // 2 if epilogue == "glu" else tn
    n_out = n // 2 if epilogue == "glu" else n
    return pl.pallas_call(
        functools.partial(_proj_kernel, epilogue=epilogue, q_scale=q_scale),
        out_shape=jax.ShapeDtypeStruct((m, n_out), out_dtype),
        grid=(n // tn, m // tm),
        in_specs=[pl.BlockSpec((tm, k), lambda j, i: (i, 0)),
                  pl.BlockSpec((k, tn), lambda j, i: (0, j))],
        out_specs=pl.BlockSpec((tm, tn_out), lambda j, i: (i, j)),
        compiler_params=_cparams(2),
        name="proj_" + epilogue,
    )(h, w)


def _gate_transform(x, exp_a, dt_bias, f_bias, idx):
    beta = _sigmoid(x)
    log_a = -exp_a * _softplus(x + dt_bias)
    log_f = -_softplus(-(x + f_bias))
    return jnp.where(idx < 8, beta, jnp.where(idx < 16, log_a, log_f))


def _small_kernel(h_ref, w_ref, wt_ref, pc_ref, pr_ref, o_ref, ot_ref, oc_ref):
    h = h_ref[...]
    col = _dot(h, w_ref[...])
    row = _dot_nt(wt_ref[...], h)
    pc = pc_ref[...]
    cidx = lax.broadcasted_iota(jnp.int32, col.shape, 1)
    o_ref[...] = _gate_transform(col, pc[0:1, :], pc[1:2, :], pc[2:3, :], cidx)
    pr = pr_ref[...]
    ridx = lax.broadcasted_iota(jnp.int32, row.shape, 0)
    row = _gate_transform(row, pr[:, 0:1], pr[:, 1:2], pr[:, 2:3], ridx)
    ot_ref[...] = row
    for c in range(row.shape[1] // CHUNK):
        oc_ref[c] = row[:, c * CHUNK:(c + 1) * CHUNK]


def _small_proj(h, w_small, w_small_t, prm_c, prm_r, tm):
    m, k = h.shape
    return pl.pallas_call(
        _small_kernel,
        out_shape=(jax.ShapeDtypeStruct((m, N_SMALL), F32),
                   jax.ShapeDtypeStruct((N_SMALL_T, m), F32),
                   jax.ShapeDtypeStruct((m // CHUNK, N_SMALL_T, CHUNK), F32)),
        grid=(m // tm,),
        in_specs=[pl.BlockSpec((tm, k), lambda i: (i, 0)),
                  pl.BlockSpec((k, N_SMALL), lambda i: (0, 0)),
                  pl.BlockSpec((N_SMALL_T, k), lambda i: (0, 0)),
                  pl.BlockSpec((8, N_SMALL), lambda i: (0, 0)),
                  pl.BlockSpec((N_SMALL_T, 128), lambda i: (0, 0))],
        out_specs=(pl.BlockSpec((tm, N_SMALL), lambda i: (i, 0)),
                   pl.BlockSpec((N_SMALL_T, tm), lambda i: (0, i)),
                   pl.BlockSpec((tm // CHUNK, N_SMALL_T, CHUNK), lambda i: (i, 0, 0))),
        compiler_params=_cparams(1),
        name="small_proj",
    )(h, w_small, w_small_t, prm_c, prm_r)


def _cumsum_kernel(x_ref, o_ref):
    s = x_ref.shape[1]
    r = lax.broadcasted_iota(jnp.int32, (128, 128), 0)
    c = lax.broadcasted_iota(jnp.int32, (128, 128), 1)
    triu = (r <= c).astype(F32)
    carry = jnp.zeros((8, 1), F32)
    for j in range(s // 128):
        blk = _dot_f32(x_ref[:, j * 128:(j + 1) * 128], triu) + carry
        o_ref[0, :, j * 128:(j + 1) * 128] = blk
        carry = blk[:, 127:128]


def _forget_cumsum(small_t, bsz, seq):
    return pl.pallas_call(
        _cumsum_kernel,
        out_shape=jax.ShapeDtypeStruct((bsz, 8, seq), F32),
        grid=(bsz,),
        in_specs=[pl.BlockSpec((8, seq), lambda b: (2, b))],
        out_specs=pl.BlockSpec((1, 8, seq), lambda b: (b, 0, 0)),
        compiler_params=_cparams(1),
        name="forget_cumsum",
    )(small_t)


def _neumann_inverse(a):
    n = a.shape[0]
    r = lax.broadcasted_iota(jnp.int32, (n, n), 0)
    c = lax.broadcasted_iota(jnp.int32, (n, n), 1)
    x = jnp.where(r == c, 1.0, 0.0).astype(F32) + a
    p = a
    steps = 1
    while 2 * steps < n:
        p = _dot(p.astype(BF16), p.astype(BF16))
        x = x + _dot(x.astype(BF16), p.astype(BF16))
        steps *= 2
    return x


def _gdn_local_kernel(qkv_ref, halo_ref, cw_ref, sc_ref, sr_ref,
                      u0_ref, w_ref, qd_ref, kt_ref, qk_ref, dec_ref, xx_ref, act_ref):
    i = pl.program_id(1)
    ts = qkv_ref.shape[1]
    bw = u0_ref.shape[2]
    nh = bw // HEAD_DIM
    halo = halo_ref[0]
    xx_ref[0:8, :] = jnp.where(i > 0, halo, 0.0)
    xx_ref[8:, :] = qkv_ref[0]

    cw = cw_ref[...]
    for rb in range(ts // CHUNK):
        acc = None
        for k in range(SHORT_CONV):
            start = rb * CHUNK + (8 - (SHORT_CONV - 1)) + k
            term = cw[k:k + 1, :] * xx_ref[start:start + CHUNK, :]
            acc = term if acc is None else acc + term
        act_ref[rb * CHUNK:(rb + 1) * CHUNK, :] = _silu(acc)

    r = lax.broadcasted_iota(jnp.int32, (CHUNK, CHUNK), 0)
    c = lax.broadcasted_iota(jnp.int32, (CHUNK, CHUNK), 1)
    incl = r >= c
    strict = r > c
    tril = incl.astype(F32)
    triu = (r <= c).astype(F32)

    def chunk_body(ci, _):
        base = pl.multiple_of(ci * CHUNK, CHUNK)
        sc = sc_ref[0, pl.ds(base, CHUNK), :]
        g_col = _dot_f32(tril, sc)
        g_row = _dot_f32(sr_ref[ci], triu)

        for h in range(nh):
            q = act_ref[pl.ds(base, CHUNK), h * HEAD_DIM:(h + 1) * HEAD_DIM]
            k_ = act_ref[pl.ds(base, CHUNK), bw + h * HEAD_DIM: bw + (h + 1) * HEAD_DIM]
            v = act_ref[pl.ds(base, CHUNK), 2 * bw + h * HEAD_DIM: 2 * bw + (h + 1) * HEAD_DIM]
            q = q * (lax.rsqrt(jnp.sum(q * q, axis=-1, keepdims=True) + NORM_EPS) * HEAD_DIM ** -0.5)
            k_ = k_ * lax.rsqrt(jnp.sum(k_ * k_, axis=-1, keepdims=True) + NORM_EPS)
            beta = sc[:, h:h + 1]
            g = g_col[:, 8 + h:9 + h]
            g_last = g_col[CHUNK - 1:CHUNK, 8 + h:9 + h]
            gr = g_row[8 + h:9 + h, :]
            decay = jnp.exp(jnp.where(incl, g - gr, -jnp.inf))
            kb = k_.astype(BF16)
            kk = _dot_nt(kb, kb)
            neg_l = jnp.where(strict, -(beta * kk * decay), 0.0)
            t_inv = _neumann_inverse(neg_l).astype(BF16)
            eg = jnp.exp(g)
            u0 = _dot(t_inv, (v * beta).astype(BF16))
            w = _dot(t_inv, (k_ * (beta * eg)).astype(BF16))
            qk = _dot_nt(q.astype(BF16), kb) * decay
            sl = slice(h * HEAD_DIM, (h + 1) * HEAD_DIM)
            u0_ref[0, pl.ds(base, CHUNK), sl] = u0
            w_ref[0, pl.ds(base, CHUNK), sl] = w.astype(BF16)
            qd_ref[0, pl.ds(base, CHUNK), sl] = (q * eg).astype(BF16)
            kt_ref[0, pl.ds(base, CHUNK), sl] = (k_ * jnp.exp(g_last - g)).astype(BF16)
            qk_ref[0, pl.ds(base, CHUNK), h * CHUNK:(h + 1) * CHUNK] = qk.astype(BF16)
            dec_ref[0, ci, :, sl] = jnp.broadcast_to(jnp.exp(g_last), (1, HEAD_DIM))
        return 0

    lax.fori_loop(0, ts // CHUNK, chunk_body, 0)


def _gdn_local(qkv_a, conv_w, small_c, small_ch, ts):
    bsz, seq, n3 = qkv_a.shape
    bw = n3 // 3
    nh = bw // HEAD_DIM
    nt = seq // ts
    nc = ts // CHUNK
    big = lambda b, i: (b, i, 0)
    return pl.pallas_call(
        _gdn_local_kernel,
        out_shape=(jax.ShapeDtypeStruct((bsz, seq, bw), F32),
                   jax.ShapeDtypeStruct((bsz, seq, bw), BF16),
                   jax.ShapeDtypeStruct((bsz, seq, bw), BF16),
                   jax.ShapeDtypeStruct((bsz, seq, bw), BF16),
                   jax.ShapeDtypeStruct((bsz, seq, nh * CHUNK), BF16),
                   jax.ShapeDtypeStruct((bsz, seq // CHUNK, 1, bw), F32)),
        grid=(bsz, nt),
        in_specs=[pl.BlockSpec((1, ts, n3), big),
                  pl.BlockSpec((1, 8, n3), lambda b, i: (b, jnp.maximum(i * (ts // 8) - 1, 0), 0)),
                  pl.BlockSpec((SHORT_CONV, n3), lambda b, i: (0, 0)),
                  pl.BlockSpec((1, ts, N_SMALL), big),
                  pl.BlockSpec((nc, N_SMALL_T, CHUNK), lambda b, i: (b * nt + i, 0, 0))],
        out_specs=(pl.BlockSpec((1, ts, bw), big),
                   pl.BlockSpec((1, ts, bw), big),
                   pl.BlockSpec((1, ts, bw), big),
                   pl.BlockSpec((1, ts, bw), big),
                   pl.BlockSpec((1, ts, nh * CHUNK), big),
                   pl.BlockSpec((1, nc, 1, bw), lambda b, i: (b, i, 0, 0))),
        scratch_shapes=[pltpu.VMEM((ts + 8, n3), F32), pltpu.VMEM((ts, n3), F32)],
        compiler_params=_cparams(2),
        name="gdn_local",
    )(qkv_a, qkv_a, conv_w, small_c.reshape(bsz, seq, N_SMALL), small_ch)


def _gdn_scan_kernel(u0_ref, w_ref, qd_ref, kt_ref, qk_ref, dec_ref, z_ref, nw_ref, y_ref, state_ref):
    i = pl.program_id(1)
    ts = u0_ref.shape[1]
    nh = u0_ref.shape[2] // HEAD_DIM

    @pl.when(i == 0)
    def _():
        state_ref[...] = jnp.zeros_like(state_ref)

    nw = nw_ref[...]

    def chunk_body(ci, _):
        base = pl.multiple_of(ci * CHUNK, CHUNK)
        rows = pl.ds(base, CHUNK)
        for h in range(nh):
            sl = slice(h * HEAD_DIM, (h + 1) * HEAD_DIM)
            state = state_ref[h]
            sb = state.astype(BF16)
            kt = kt_ref[0, rows, sl]
            u = u0_ref[0, rows, sl] - _dot(w_ref[0, rows, sl], sb)
            ub = u.astype(BF16)
            o = _dot(qd_ref[0, rows, sl], sb) + _dot(qk_ref[0, rows, h * CHUNK:(h + 1) * CHUNK], ub)
            dec = dec_ref[0, ci, :, sl]
            state_ref[h] = state * dec + _dot_tn(kt, ub)
            o = o * lax.rsqrt(jnp.mean(o * o, axis=-1, keepdims=True) + NORM_EPS)
            y_ref[0, rows, sl] = (o * nw * z_ref[0, rows, sl].astype(F32)).astype(BF16)
        return 0

    lax.fori_loop(0, ts // CHUNK, chunk_body, 0)


def _gdn_scan(u0, w, qd, kt, qk, dec, z_all, o_norm_w, ts):
    bsz, seq, bw = u0.shape
    nh = bw // HEAD_DIM
    big = lambda b, i: (b, i, 0)
    return pl.pallas_call(
        _gdn_scan_kernel,
        out_shape=jax.ShapeDtypeStruct((bsz, seq, bw), BF16),
        grid=(bsz, seq // ts),
        in_specs=[pl.BlockSpec((1, ts, bw), big),
                  pl.BlockSpec((1, ts, bw), big),
                  pl.BlockSpec((1, ts, bw), big),
                  pl.BlockSpec((1, ts, bw), big),
                  pl.BlockSpec((1, ts, nh * CHUNK), big),
                  pl.BlockSpec((1, ts // CHUNK, 1, bw), lambda b, i: (b, i, 0, 0)),
                  pl.BlockSpec((1, ts, bw), big),
                  pl.BlockSpec((1, HEAD_DIM), lambda b, i: (0, 0))],
        out_specs=pl.BlockSpec((1, ts, bw), big),
        scratch_shapes=[pltpu.VMEM((nh, HEAD_DIM, HEAD_DIM), F32)],
        compiler_params=_cparams(2),
        name="gdn_scan",
    )(u0, w, qd, kt, qk, dec, z_all, o_norm_w.reshape(1, HEAD_DIM))


def _conformer_kernel(u_ref, halo_ref, z_ref, cw_ref, cb_ref, lw_ref, lb_ref, y_ref, xx_ref):
    i = pl.program_id(1)
    ts = u_ref.shape[1]
    pad = halo_ref.shape[1]
    xx_ref[0:pad, :] = jnp.where(i > 0, halo_ref[0], 0.0)
    xx_ref[pad:, :] = u_ref[0]
    cw = cw_ref[...]
    acc = None
    for k in range(CONF_CONV):
        term = cw[k:k + 1, :] * xx_ref[pad - (CONF_CONV - 1) + k: pad - (CONF_CONV - 1) + k + ts, :]
        acc = term if acc is None else acc + term
    u = acc + cb_ref[...]
    xc = u - jnp.mean(u, axis=-1, keepdims=True)
    y = xc * lax.rsqrt(jnp.mean(xc * xc, axis=-1, keepdims=True) + NORM_EPS)
    y = _silu(y * lw_ref[...] + lb_ref[...])
    y_ref[0] = (y * z_ref[0].astype(F32)).astype(BF16)


def _conformer(u, z_all, conv_w, conv_b, ln_w, ln_b, ts):
    bsz, seq, bw = u.shape
    pad = 32
    big = lambda b, i: (b, i, 0)
    vec = lambda b, i: (0, 0)
    return pl.pallas_call(
        _conformer_kernel,
        out_shape=jax.ShapeDtypeStruct((bsz, seq, bw), BF16),
        grid=(bsz, seq // ts),
        in_specs=[pl.BlockSpec((1, ts, bw), big),
                  pl.BlockSpec((1, pad, bw), lambda b, i: (b, jnp.maximum(i * (ts // pad) - 1, 0), 0)),
                  pl.BlockSpec((1, ts, bw), lambda b, i: (b, i, 1)),
                  pl.BlockSpec((CONF_CONV, bw), vec),
                  pl.BlockSpec((1, bw), vec),
                  pl.BlockSpec((1, bw), vec),
                  pl.BlockSpec((1, bw), vec)],
        out_specs=pl.BlockSpec((1, ts, bw), big),
        scratch_shapes=[pltpu.VMEM((ts + pad, bw), F32)],
        compiler_params=_cparams(2),
        name="conformer",
    )(u, u, z_all, conv_w, conv_b.reshape(1, bw), ln_w.reshape(1, bw), ln_b.reshape(1, bw))


def _fox_kernel(q_ref, k_ref, v_ref, c_ref, z_ref, y_ref, m_ref, l_ref, acc_ref):
    i = pl.program_id(2)
    j = pl.program_id(3)
    tq = q_ref.shape[1]
    tk = k_ref.shape[1]

    @pl.when(j == 0)
    def _():
        m_ref[...] = jnp.full_like(m_ref, -jnp.inf)
        l_ref[...] = jnp.zeros_like(l_ref)
        acc_ref[...] = jnp.zeros_like(acc_ref)

    def step(masked):
        s = _dot_nt(q_ref[0], k_ref[0]) - c_ref[0]
        if masked:
            r = lax.broadcasted_iota(jnp.int32, (tq, tk), 0)
            c = lax.broadcasted_iota(jnp.int32, (tq, tk), 1)
            s = jnp.where(r >= c, s, -jnp.inf)
        m_prev = m_ref[...]
        m_new = jnp.maximum(m_prev, jnp.max(s, axis=-1, keepdims=True))
        p = jnp.exp(s - m_new)
        alpha = jnp.exp(m_prev - m_new)
        l_ref[...] = alpha * l_ref[...] + jnp.sum(p, axis=-1, keepdims=True)
        acc_ref[...] = alpha * acc_ref[...] + _dot(p.astype(BF16), v_ref[0])
        m_ref[...] = m_new

    @pl.when(j < i)
    def _():
        step(False)

    @pl.when(j == i)
    def _():
        step(True)
        o = acc_ref[...] / l_ref[...]
        y_ref[0] = (o * z_ref[0].astype(F32)).astype(BF16)


def _fox_attention(qkv_c, c_key, z_all, t):
    bsz, seq, n3 = qkv_c.shape
    nh = n3 // 3 // HEAD_DIM
    nt = seq // t
    kv = lambda off: (lambda b, h, i, j: (b, jnp.minimum(j, i), off + h))
    return pl.pallas_call(
        _fox_kernel,
        out_shape=jax.ShapeDtypeStruct((bsz, seq, nh * HEAD_DIM), BF16),
        grid=(bsz, nh, nt, nt),
        in_specs=[pl.BlockSpec((1, t, HEAD_DIM), lambda b, h, i, j: (b, i, h)),
                  pl.BlockSpec((1, t, HEAD_DIM), kv(nh)),
                  pl.BlockSpec((1, t, HEAD_DIM), kv(2 * nh)),
                  pl.BlockSpec((1, 1, t), lambda b, h, i, j: (b * nh + h, 0, jnp.minimum(j, i))),
                  pl.BlockSpec((1, t, HEAD_DIM), lambda b, h, i, j: (b, i, 2 * nh + h))],
        out_specs=pl.BlockSpec((1, t, HEAD_DIM), lambda b, h, i, j: (b, i, h)),
        scratch_shapes=[pltpu.VMEM((t, 1), F32), pltpu.VMEM((t, 1), F32), pltpu.VMEM((t, HEAD_DIM), F32)],
        compiler_params=_cparams(4),
        name="fox_attention",
    )(qkv_c, qkv_c, qkv_c, c_key.reshape(bsz * nh, 1, seq), z_all)


def _merge_kernel(h_ref, ya_ref, yb_ref, yc_ref, wg0_ref, wg1_ref, wg2_ref, wb0_ref, wb1_ref, wb2_ref, o_ref):
    h = h_ref[...]
    acc = None
    for y_ref, wg_ref, wb_ref in ((ya_ref, wg0_ref, wb0_ref), (yb_ref, wg1_ref, wb1_ref), (yc_ref, wg2_ref, wb2_ref)):
        term = _sigmoid(_dot(h, wg_ref[...])) * _dot(y_ref[...], wb_ref[0])
        acc = term if acc is None else acc + term
    o_ref[...] = acc.astype(BF16)


def _merge(h, ya, yb, yc, w_gate, w_branch, tm, tn):
    m, d = h.shape
    bw = ya.shape[1]
    nj = d // tn
    row = lambda j, i: (i, 0)
    return pl.pallas_call(
        _merge_kernel,
        out_shape=jax.ShapeDtypeStruct((m, d), BF16),
        grid=(nj, m // tm),
        in_specs=[pl.BlockSpec((tm, d), row),
                  pl.BlockSpec((tm, bw), row), pl.BlockSpec((tm, bw), row), pl.BlockSpec((tm, bw), row),
                  pl.BlockSpec((d, tn), lambda j, i: (0, j)),
                  pl.BlockSpec((d, tn), lambda j, i: (0, nj + j)),
                  pl.BlockSpec((d, tn), lambda j, i: (0, 2 * nj + j)),
                  pl.BlockSpec((1, bw, tn), lambda j, i: (0, 0, j)),
                  pl.BlockSpec((1, bw, tn), lambda j, i: (1, 0, j)),
                  pl.BlockSpec((1, bw, tn), lambda j, i: (2, 0, j))],
        out_specs=pl.BlockSpec((tm, tn), lambda j, i: (i, j)),
        compiler_params=_cparams(2),
        name="gated_merge",
    )(h, ya, yb, yc, w_gate, w_gate, w_gate, w_branch, w_branch, w_branch)


def _out_kernel(m_ref, w_ref, x_ref, pw_ref, o_ref):
    out = _dot(m_ref[...], w_ref[...])
    y = out * lax.rsqrt(jnp.mean(out * out, axis=-1, keepdims=True) + NORM_EPS)
    o_ref[...] = x_ref[...] + y * pw_ref[...]


def _out_proj(merged, w_out, x2, post_w, tm):
    m, d = x2.shape
    return pl.pallas_call(
        _out_kernel,
        out_shape=jax.ShapeDtypeStruct((m, d), F32),
        grid=(m // tm,),
        in_specs=[pl.BlockSpec((tm, d), lambda i: (i, 0)),
                  pl.BlockSpec((d, d), lambda i: (0, 0)),
                  pl.BlockSpec((tm, d), lambda i: (i, 0)),
                  pl.BlockSpec((1, d), lambda i: (0, 0))],
        out_specs=pl.BlockSpec((tm, d), lambda i: (i, 0)),
        compiler_params=_cparams(1),
        name="out_proj",
    )(merged, w_out, x2, post_w.reshape(1, d))


def _layer(x2, bsz, seq, p):
    m, d = x2.shape
    bw = d // 2
    tm = min(1024, m)
    h = _prenorm(x2, p["pre_w"], min(512, m))

    qkv_a = _proj(h, p["w_qkv_a"], epilogue="f32", tm=tm, tn=1024, out_dtype=F32)
    z_all = _proj(h, p["w_z"], epilogue="silu_bf16", tm=tm, tn=1024, out_dtype=BF16)
    u_b = _proj(h, p["w_glu"], epilogue="glu", tm=tm, tn=1024, out_dtype=F32)
    qkv_c = _proj(h, p["w_qkv_c"], epilogue="qkv_bf16", tm=tm, tn=1024, out_dtype=BF16,
                  q_scale=HEAD_DIM ** -0.5)
    small_c, small_t, small_ch = _small_proj(h, p["w_small"], p["w_small_t"], p["prm_c"], p["prm_r"],
                                             min(512, seq))

    z3 = z_all.reshape(bsz, seq, 3 * bw)
    u0, w, qd, kt, qk, dec = _gdn_local(qkv_a.reshape(bsz, seq, 3 * bw), p["conv_qkv_w"], small_c, small_ch,
                                        min(256, seq))
    y_a = _gdn_scan(u0, w, qd, kt, qk, dec, z3, p["o_norm_w"], min(512, seq))

    y_b = _conformer(u_b.reshape(bsz, seq, bw), z3, p["conv_w"], p["conv_b"], p["ln_w"], p["ln_b"], min(256, seq))

    c_key = _forget_cumsum(small_t, bsz, seq)
    y_c = _fox_attention(qkv_c.reshape(bsz, seq, 3 * bw), c_key, z3, min(512, seq))

    merged = _merge(h, y_a.reshape(m, bw), y_b.reshape(m, bw), y_c.reshape(m, bw),
                    p["w_gate"], p["w_branch"], min(512, m), 512)
    return _out_proj(merged, p["w_out"], x2, p["post_w"], min(256, m))


def _prep_params(pre_norm_w, post_norm_w, w_in, conv_qkv_w, a_log, dt_bias, o_norm_w,
                 conv_w, conv_b, ln_w, ln_b, f_bias, w_branch, w_out):
    depth, d, _ = w_in.shape
    bw = d // 2
    nh = bw // HEAD_DIM
    sizes = (3 * bw, bw, nh, nh, 2 * bw, bw, 3 * bw, bw, nh, N_BRANCH * d)
    offs = [0]
    for s in sizes:
        offs.append(offs[-1] + s)
    seg = lambda k: w_in[:, :, offs[k]:offs[k + 1]]
    glu = seg(4)
    half = 512
    val = glu[:, :, :bw].reshape(depth, d, bw // half, half)
    gate = glu[:, :, bw:].reshape(depth, d, bw // half, half)
    w_glu = jnp.stack([val, gate], axis=3).reshape(depth, d, 2 * bw)
    w_small = jnp.concatenate([seg(2), seg(3), seg(8)], axis=2)
    w_small = jnp.pad(w_small, ((0, 0), (0, 0), (0, N_SMALL - 3 * nh))).astype(BF16)
    w_small_t = jnp.swapaxes(w_small[:, :, :N_SMALL_T], 1, 2)
    zeros = jnp.zeros((depth, nh), F32)
    ones = jnp.ones((depth, nh), F32)
    vec24 = lambda a, b, c: jnp.concatenate([a, b, c], axis=1)
    prm = jnp.stack([vec24(ones, jnp.exp(a_log.astype(F32)), ones),
                     vec24(zeros, dt_bias.astype(F32), zeros),
                     vec24(zeros, zeros, f_bias.astype(F32))], axis=1)
    prm_c = jnp.pad(prm, ((0, 0), (0, 8 - 3), (0, N_SMALL - 3 * nh)))
    prm_r = jnp.pad(jnp.swapaxes(prm, 1, 2), ((0, 0), (0, N_SMALL_T - 3 * nh), (0, 128 - 3)))
    return {
        "pre_w": pre_norm_w, "post_w": post_norm_w,
        "w_qkv_a": seg(0).astype(BF16),
        "w_z": jnp.concatenate([seg(1), seg(5), seg(7)], axis=2).astype(BF16),
        "w_glu": w_glu.astype(BF16),
        "w_qkv_c": seg(6).astype(BF16),
        "w_gate": seg(9).astype(BF16),
        "w_small": w_small, "w_small_t": w_small_t, "prm_c": prm_c, "prm_r": prm_r,
        "conv_qkv_w": conv_qkv_w, "o_norm_w": o_norm_w,
        "conv_w": conv_w, "conv_b": conv_b, "ln_w": ln_w, "ln_b": ln_b,
        "w_branch": w_branch.astype(BF16), "w_out": w_out.astype(BF16),
    }


def kernel(x, pre_norm_w, post_norm_w, w_in, conv_qkv_w, a_log, dt_bias, o_norm_w, conv_w, conv_b, ln_w, ln_b, f_bias, w_branch, w_out):
    bsz, seq, d = x.shape
    depth = w_in.shape[0]
    params = _prep_params(pre_norm_w, post_norm_w, w_in, conv_qkv_w, a_log, dt_bias, o_norm_w,
                          conv_w, conv_b, ln_w, ln_b, f_bias, w_branch, w_out)
    x2 = x.reshape(bsz * seq, d)
    for l in range(depth):
        x2 = _layer(x2, bsz, seq, {k: v[l] for k, v in params.items()})
    return x2.reshape(bsz, seq, d)
```

```python
import functools

import jax
import jax.numpy as jnp
from jax import lax
from jax.experimental import pallas as pl
from jax.experimental.pallas import tpu as pltpu

F32 = jnp.float32
BF16 = jnp.bfloat16

HEAD_DIM = 128
CHUNK = 64
SHORT_CONV = 4
CONF_CONV = 31
NORM_EPS = 1e-6
N_BRANCH = 3
N_SMALL = 128
N_SMALL_T = 32
V7X_VMEM_LIMIT = 48 * 1024 * 1024


def _cparams(n_axes):
    return pltpu.CompilerParams(dimension_semantics=("arbitrary",) * n_axes,
                                vmem_limit_bytes=V7X_VMEM_LIMIT)


def _sigmoid(x):
    return 1.0 / (1.0 + jnp.exp(-x))


def _silu(x):
    return x * _sigmoid(x)


def _softplus(x):
    return jnp.maximum(x, 0.0) + jnp.log(1.0 + jnp.exp(-jnp.abs(x)))


def _dot(a, b):
    return jnp.dot(a, b, preferred_element_type=F32)


def _dot_nt(a, b):
    return lax.dot_general(a, b, (((1,), (1,)), ((), ())), preferred_element_type=F32)


def _dot_tn(a, b):
    return lax.dot_general(a, b, (((0,), (0,)), ((), ())), preferred_element_type=F32)


def _dot_f32(a, b):
    return jnp.dot(a, b, preferred_element_type=F32, precision=lax.Precision.HIGHEST)


def _prenorm_kernel(x_ref, w_ref, h_ref):
    x = x_ref[...]
    y = x * lax.rsqrt(jnp.mean(x * x, axis=-1, keepdims=True) + NORM_EPS)
    h_ref[...] = (y * w_ref[...]).astype(BF16)


def _prenorm(x2, w, tm):
    m, d = x2.shape
    return pl.pallas_call(
        _prenorm_kernel,
        out_shape=jax.ShapeDtypeStruct((m, d), BF16),
        grid=(m // tm,),
        in_specs=[pl.BlockSpec((tm, d), lambda i: (i, 0)),
                  pl.BlockSpec((1, d), lambda i: (0, 0))],
        out_specs=pl.BlockSpec((tm, d), lambda i: (i, 0)),
        compiler_params=_cparams(1),
        name="prenorm",
    )(x2, w.reshape(1, d))


def _proj_kernel(h_ref, w_ref, o_ref, *, epilogue):
    acc = _dot(h_ref[...], w_ref[...])
    if epilogue == "f32":
        o_ref[...] = acc
    elif epilogue == "silu_bf16":
        o_ref[...] = _silu(acc).astype(BF16)
    elif epilogue == "glu":
        half = acc.shape[1] // 2
        o_ref[...] = acc[:, :half] * _sigmoid(acc[:, half:])
    elif epilogue == "bf16":
        o_ref[...] = acc.astype(BF16)
    else:
        raise ValueError(epilogue)


def _proj_t_kernel(wt_ref, h_ref, o_ref, *, scale):
    o_ref[0] = (_dot_nt(wt_ref[...], h_ref[...]) * scale).astype(BF16)


def _proj_t(h, w_t, *, bsz, tm, tn, scale=1.0):
    m, k = h.shape
    n = w_t.shape[0]
    seq = m // bsz
    nts = seq // tm
    return pl.pallas_call(
        functools.partial(_proj_t_kernel, scale=scale),
        out_shape=jax.ShapeDtypeStruct((bsz, n, seq), BF16),
        grid=(n // tn, m // tm),
        in_specs=[pl.BlockSpec((tn, k), lambda j, i: (j, 0)),
                  pl.BlockSpec((tm, k), lambda j, i: (i, 0))],
        out_specs=pl.BlockSpec((1, tn, tm), lambda j, i: (i // nts, j, i % nts)),
        compiler_params=_cparams(2),
        name="proj_t",
    )(w_t, h)


def _proj(h, w, *, epilogue, tm, tn, out_dtype):
    m, k = h.shape
    n = w.shape[1]
    tn_out = tn // 2 if epilogue == "glu" else tn
    n_out = n // 2 if epilogue == "glu" else n
    return pl.pallas_call(
        functools.partial(_proj_kernel, epilogue=epilogue),
        out_shape=jax.ShapeDtypeStruct((m, n_out), out_dtype),
        grid=(n // tn, m // tm),
        in_specs=[pl.BlockSpec((tm, k), lambda j, i: (i, 0)),
                  pl.BlockSpec((k, tn), lambda j, i: (0, j))],
        out_specs=pl.BlockSpec((tm, tn_out), lambda j, i: (i, j)),
        compiler_params=_cparams(2),
        name="proj_" + epilogue,
    )(h, w)


def _gate_transform(x, exp_a, dt_bias, f_bias, idx):
    beta = _sigmoid(x)
    log_a = -exp_a * _softplus(x + dt_bias)
    log_f = -_softplus(-(x + f_bias))
    return jnp.where(idx < 8, beta, jnp.where(idx < 16, log_a, log_f))


def _small_kernel(h_ref, w_ref, wt_ref, pc_ref, pr_ref, o_ref, ot_ref, oc_ref):
    h = h_ref[...]
    col = _dot(h, w_ref[...])
    row = _dot_nt(wt_ref[...], h)
    pc = pc_ref[...]
    cidx = lax.broadcasted_iota(jnp.int32, col.shape, 1)
    o_ref[...] = _gate_transform(col, pc[0:1, :], pc[1:2, :], pc[2:3, :], cidx)
    pr = pr_ref[...]
    ridx = lax.broadcasted_iota(jnp.int32, row.shape, 0)
    row = _gate_transform(row, pr[:, 0:1], pr[:, 1:2], pr[:, 2:3], ridx)
    ot_ref[...] = row
    for c in range(row.shape[1] // CHUNK):
        oc_ref[c] = row[:, c * CHUNK:(c + 1) * CHUNK]


def _small_proj(h, w_small, w_small_t, prm_c, prm_r, tm):
    m, k = h.shape
    return pl.pallas_call(
        _small_kernel,
        out_shape=(jax.ShapeDtypeStruct((m, N_SMALL), F32),
                   jax.ShapeDtypeStruct((N_SMALL_T, m), F32),
                   jax.ShapeDtypeStruct((m // CHUNK, N_SMALL_T, CHUNK), F32)),
        grid=(m // tm,),
        in_specs=[pl.BlockSpec((tm, k), lambda i: (i, 0)),
                  pl.BlockSpec((k, N_SMALL), lambda i: (0, 0)),
                  pl.BlockSpec((N_SMALL_T, k), lambda i: (0, 0)),
                  pl.BlockSpec((8, N_SMALL), lambda i: (0, 0)),
                  pl.BlockSpec((N_SMALL_T, 128), lambda i: (0, 0))],
        out_specs=(pl.BlockSpec((tm, N_SMALL), lambda i: (i, 0)),
                   pl.BlockSpec((N_SMALL_T, tm), lambda i: (0, i)),
                   pl.BlockSpec((tm // CHUNK, N_SMALL_T, CHUNK), lambda i: (i, 0, 0))),
        compiler_params=_cparams(1),
        name="small_proj",
    )(h, w_small, w_small_t, prm_c, prm_r)


def _split3(x):
    a = x.astype(BF16)
    r1 = x - a.astype(F32)
    b = r1.astype(BF16)
    c = (r1 - b.astype(F32)).astype(BF16)
    return a, b, c


def _cumsum_kernel(x_ref, o_ref, carry_ref):
    ts = x_ref.shape[1]
    nh = o_ref.shape[1]
    blk = 128

    @pl.when(pl.program_id(1) == 0)
    def _():
        carry_ref[...] = jnp.zeros_like(carry_ref)

    r = lax.broadcasted_iota(jnp.int32, (blk, blk), 0)
    c = lax.broadcasted_iota(jnp.int32, (blk, blk), 1)
    tril = jnp.where(r >= c, 1.0, 0.0).astype(BF16)
    er = lax.broadcasted_iota(jnp.int32, (blk, nh * blk), 0)
    ec = lax.broadcasted_iota(jnp.int32, (blk, nh * blk), 1)
    spread = jnp.where(er == 2 * nh + ec // blk, 1.0, 0.0).astype(BF16)

    carry = carry_ref[0:1, :]
    for j in range(ts // blk):
        rows = slice(j * blk, (j + 1) * blk)
        x1, x2, x3 = _split3(x_ref[0, rows, :])
        run = _dot(tril, x1) + _dot(tril, x2) + _dot(tril, x3) + carry
        carry = run[blk - 1:blk, :]
        c1, c2, c3 = _split3(run)
        rep = _dot(c1, spread) + _dot(c2, spread) + _dot(c3, spread)
        for h in range(nh):
            o_ref[0, h, rows, :] = rep[:, h * blk:(h + 1) * blk]
    carry_ref[0:1, :] = carry


def _forget_cumsum(small_c3, nh, ts):
    bsz, seq, _ = small_c3.shape
    return pl.pallas_call(
        _cumsum_kernel,
        out_shape=jax.ShapeDtypeStruct((bsz, nh, seq, 128), F32),
        grid=(bsz, seq // ts),
        in_specs=[pl.BlockSpec((1, ts, N_SMALL), lambda b, i: (b, i, 0))],
        out_specs=pl.BlockSpec((1, nh, ts, 128), lambda b, i: (b, 0, i, 0)),
        scratch_shapes=[pltpu.VMEM((8, N_SMALL), F32)],
        compiler_params=_cparams(2),
        name="forget_cumsum",
    )(small_c3)


def _neumann_inverse(a):
    n = a.shape[0]
    r = lax.broadcasted_iota(jnp.int32, (n, n), 0)
    c = lax.broadcasted_iota(jnp.int32, (n, n), 1)
    x = jnp.where(r == c, 1.0, 0.0).astype(F32) + a
    p = a
    steps = 1
    while 2 * steps < n:
        p = _dot(p.astype(BF16), p.astype(BF16))
        x = x + _dot(x.astype(BF16), p.astype(BF16))
        steps *= 2
    return x


def _gdn_local_kernel(qkv_ref, halo_ref, cw_ref, sc_ref, sr_ref,
                      u0_ref, w_ref, qd_ref, kt_ref, qk_ref, dec_ref, xx_ref, act_ref):
    i = pl.program_id(1)
    ts = qkv_ref.shape[1]
    bw = u0_ref.shape[2]
    nh = bw // HEAD_DIM
    halo = halo_ref[0]
    xx_ref[0:8, :] = jnp.where(i > 0, halo, 0.0)
    xx_ref[8:, :] = qkv_ref[0]

    cw = cw_ref[...]
    for rb in range(ts // CHUNK):
        acc = None
        for k in range(SHORT_CONV):
            start = rb * CHUNK + (8 - (SHORT_CONV - 1)) + k
            term = cw[k:k + 1, :] * xx_ref[start:start + CHUNK, :]
            acc = term if acc is None else acc + term
        act_ref[rb * CHUNK:(rb + 1) * CHUNK, :] = _silu(acc)

    r = lax.broadcasted_iota(jnp.int32, (CHUNK, CHUNK), 0)
    c = lax.broadcasted_iota(jnp.int32, (CHUNK, CHUNK), 1)
    incl = r >= c
    strict = r > c
    tril = incl.astype(F32)
    triu = (r <= c).astype(F32)

    def chunk_body(ci, _):
        base = pl.multiple_of(ci * CHUNK, CHUNK)
        sc = sc_ref[0, pl.ds(base, CHUNK), :]
        g_col = _dot_f32(tril, sc)
        g_row = _dot_f32(sr_ref[ci], triu)

        for h in range(nh):
            q = act_ref[pl.ds(base, CHUNK), h * HEAD_DIM:(h + 1) * HEAD_DIM]
            k_ = act_ref[pl.ds(base, CHUNK), bw + h * HEAD_DIM: bw + (h + 1) * HEAD_DIM]
            v = act_ref[pl.ds(base, CHUNK), 2 * bw + h * HEAD_DIM: 2 * bw + (h + 1) * HEAD_DIM]
            q = q * (lax.rsqrt(jnp.sum(q * q, axis=-1, keepdims=True) + NORM_EPS) * HEAD_DIM ** -0.5)
            k_ = k_ * lax.rsqrt(jnp.sum(k_ * k_, axis=-1, keepdims=True) + NORM_EPS)
            beta = sc[:, h:h + 1]
            g = g_col[:, 8 + h:9 + h]
            g_last = g_col[CHUNK - 1:CHUNK, 8 + h:9 + h]
            gr = g_row[8 + h:9 + h, :]
            decay = jnp.exp(jnp.where(incl, g - gr, -jnp.inf))
            kb = k_.astype(BF16)
            kk = _dot_nt(kb, kb)
            neg_l = jnp.where(strict, -(beta * kk * decay), 0.0)
            t_inv = _neumann_inverse(neg_l).astype(BF16)
            eg = jnp.exp(g)
            u0 = _dot(t_inv, (v * beta).astype(BF16))
            w = _dot(t_inv, (k_ * (beta * eg)).astype(BF16))
            qk = _dot_nt(q.astype(BF16), kb) * decay
            sl = slice(h * HEAD_DIM, (h + 1) * HEAD_DIM)
            u0_ref[0, pl.ds(base, CHUNK), sl] = u0
            w_ref[0, pl.ds(base, CHUNK), sl] = w.astype(BF16)
            qd_ref[0, pl.ds(base, CHUNK), sl] = (q * eg).astype(BF16)
            kt_ref[0, pl.ds(base, CHUNK), sl] = (k_ * jnp.exp(g_last - g)).astype(BF16)
            qk_ref[0, pl.ds(base, CHUNK), h * CHUNK:(h + 1) * CHUNK] = qk.astype(BF16)
            dec_ref[0, ci, :, sl] = jnp.broadcast_to(jnp.exp(g_last), (1, HEAD_DIM))
        return 0

    lax.fori_loop(0, ts // CHUNK, chunk_body, 0)


def _gdn_local(qkv_a, conv_w, small_c, small_ch, ts):
    bsz, seq, n3 = qkv_a.shape
    bw = n3 // 3
    nh = bw // HEAD_DIM
    nt = seq // ts
    nc = ts // CHUNK
    big = lambda b, i: (b, i, 0)
    return pl.pallas_call(
        _gdn_local_kernel,
        out_shape=(jax.ShapeDtypeStruct((bsz, seq, bw), F32),
                   jax.ShapeDtypeStruct((bsz, seq, bw), BF16),
                   jax.ShapeDtypeStruct((bsz, seq, bw), BF16),
                   jax.ShapeDtypeStruct((bsz, seq, bw), BF16),
                   jax.ShapeDtypeStruct((bsz, seq, nh * CHUNK), BF16),
                   jax.ShapeDtypeStruct((bsz, seq // CHUNK, 1, bw), F32)),
        grid=(bsz, nt),
        in_specs=[pl.BlockSpec((1, ts, n3), big),
                  pl.BlockSpec((1, 8, n3), lambda b, i: (b, jnp.maximum(i * (ts // 8) - 1, 0), 0)),
                  pl.BlockSpec((SHORT_CONV, n3), lambda b, i: (0, 0)),
                  pl.BlockSpec((1, ts, N_SMALL), big),
                  pl.BlockSpec((nc, N_SMALL_T, CHUNK), lambda b, i: (b * nt + i, 0, 0))],
        out_specs=(pl.BlockSpec((1, ts, bw), big),
                   pl.BlockSpec((1, ts, bw), big),
                   pl.BlockSpec((1, ts, bw), big),
                   pl.BlockSpec((1, ts, bw), big),
                   pl.BlockSpec((1, ts, nh * CHUNK), big),
                   pl.BlockSpec((1, nc, 1, bw), lambda b, i: (b, i, 0, 0))),
        scratch_shapes=[pltpu.VMEM((ts + 8, n3), F32), pltpu.VMEM((ts, n3), F32)],
        compiler_params=_cparams(2),
        name="gdn_local",
    )(qkv_a, qkv_a, conv_w, small_c.reshape(bsz, seq, N_SMALL), small_ch)


def _gdn_scan_kernel(u0_ref, w_ref, qd_ref, kt_ref, qk_ref, dec_ref, z_ref, nw_ref, y_ref, state_ref):
    i = pl.program_id(1)
    ts = u0_ref.shape[1]
    nh = u0_ref.shape[2] // HEAD_DIM

    @pl.when(i == 0)
    def _():
        state_ref[...] = jnp.zeros_like(state_ref)

    nw = nw_ref[...]

    def chunk_body(ci, _):
        base = pl.multiple_of(ci * CHUNK, CHUNK)
        rows = pl.ds(base, CHUNK)
        for h in range(nh):
            sl = slice(h * HEAD_DIM, (h + 1) * HEAD_DIM)
            state = state_ref[h]
            sb = state.astype(BF16)
            kt = kt_ref[0, rows, sl]
            u = u0_ref[0, rows, sl] - _dot(w_ref[0, rows, sl], sb)
            ub = u.astype(BF16)
            o = _dot(qd_ref[0, rows, sl], sb) + _dot(qk_ref[0, rows, h * CHUNK:(h + 1) * CHUNK], ub)
            dec = dec_ref[0, ci, :, sl]
            state_ref[h] = state * dec + _dot_tn(kt, ub)
            o = o * lax.rsqrt(jnp.mean(o * o, axis=-1, keepdims=True) + NORM_EPS)
            y_ref[0, rows, sl] = (o * nw * z_ref[0, rows, sl].astype(F32)).astype(BF16)
        return 0

    lax.fori_loop(0, ts // CHUNK, chunk_body, 0)


def _gdn_scan(u0, w, qd, kt, qk, dec, z_all, o_norm_w, ts):
    bsz, seq, bw = u0.shape
    nh = bw // HEAD_DIM
    big = lambda b, i: (b, i, 0)
    return pl.pallas_call(
        _gdn_scan_kernel,
        out_shape=jax.ShapeDtypeStruct((bsz, seq, bw), BF16),
        grid=(bsz, seq // ts),
        in_specs=[pl.BlockSpec((1, ts, bw), big),
                  pl.BlockSpec((1, ts, bw), big),
                  pl.BlockSpec((1, ts, bw), big),
                  pl.BlockSpec((1, ts, bw), big),
                  pl.BlockSpec((1, ts, nh * CHUNK), big),
                  pl.BlockSpec((1, ts // CHUNK, 1, bw), lambda b, i: (b, i, 0, 0)),
                  pl.BlockSpec((1, ts, bw), big),
                  pl.BlockSpec((1, HEAD_DIM), lambda b, i: (0, 0))],
        out_specs=pl.BlockSpec((1, ts, bw), big),
        scratch_shapes=[pltpu.VMEM((nh, HEAD_DIM, HEAD_DIM), F32)],
        compiler_params=_cparams(2),
        name="gdn_scan",
    )(u0, w, qd, kt, qk, dec, z_all, o_norm_w.reshape(1, HEAD_DIM))


def _conformer_kernel(u_ref, halo_ref, z_ref, cw_ref, cb_ref, lw_ref, lb_ref, y_ref, xx_ref):
    i = pl.program_id(1)
    ts = u_ref.shape[1]
    pad = halo_ref.shape[1]
    xx_ref[0:pad, :] = jnp.where(i > 0, halo_ref[0], 0.0)
    xx_ref[pad:, :] = u_ref[0]
    cw = cw_ref[...]
    acc = None
    for k in range(CONF_CONV):
        term = cw[k:k + 1, :] * xx_ref[pad - (CONF_CONV - 1) + k: pad - (CONF_CONV - 1) + k + ts, :]
        acc = term if acc is None else acc + term
    u = acc + cb_ref[...]
    xc = u - jnp.mean(u, axis=-1, keepdims=True)
    y = xc * lax.rsqrt(jnp.mean(xc * xc, axis=-1, keepdims=True) + NORM_EPS)
    y = _silu(y * lw_ref[...] + lb_ref[...])
    y_ref[0] = (y * z_ref[0].astype(F32)).astype(BF16)


def _conformer(u, z_all, conv_w, conv_b, ln_w, ln_b, ts):
    bsz, seq, bw = u.shape
    pad = 32
    big = lambda b, i: (b, i, 0)
    vec = lambda b, i: (0, 0)
    return pl.pallas_call(
        _conformer_kernel,
        out_shape=jax.ShapeDtypeStruct((bsz, seq, bw), BF16),
        grid=(bsz, seq // ts),
        in_specs=[pl.BlockSpec((1, ts, bw), big),
                  pl.BlockSpec((1, pad, bw), lambda b, i: (b, jnp.maximum(i * (ts // pad) - 1, 0), 0)),
                  pl.BlockSpec((1, ts, bw), lambda b, i: (b, i, 1)),
                  pl.BlockSpec((CONF_CONV, bw), vec),
                  pl.BlockSpec((1, bw), vec),
                  pl.BlockSpec((1, bw), vec),
                  pl.BlockSpec((1, bw), vec)],
        out_specs=pl.BlockSpec((1, ts, bw), big),
        scratch_shapes=[pltpu.VMEM((ts + pad, bw), F32)],
        compiler_params=_cparams(2),
        name="conformer",
    )(u, u, z_all, conv_w, conv_b.reshape(1, bw), ln_w.reshape(1, bw), ln_b.reshape(1, bw))


FOX_SUB = 256


def _fox_kernel(qi_ref, kj_ref, qt_ref, k_ref, vt_ref, c_ref, z_ref, y_ref, m_ref, l_ref, acc_ref):
    t = pl.program_id(2)
    i = qi_ref[t]
    j = kj_ref[t]
    tq = qt_ref.shape[2]
    tk = k_ref.shape[1]
    nsub = tq // FOX_SUB

    @pl.when(j == 0)
    def _():
        m_ref[...] = jnp.full_like(m_ref, -jnp.inf)
        l_ref[...] = jnp.zeros_like(l_ref)
        acc_ref[...] = jnp.zeros_like(acc_ref)

    def step(diag):
        rows = [min((n + 1) * FOX_SUB, tk) if diag else tk for n in range(nsub)]
        s_all = [_dot(k_ref[0, :rows[n], :], qt_ref[0, :, n * FOX_SUB:(n + 1) * FOX_SUB]) for n in range(nsub)]
        for n in range(nsub):
            cols = slice(n * FOX_SUB, (n + 1) * FOX_SUB)
            c = c_ref[0, 0, :rows[n], :]
            s = s_all[n] - jnp.concatenate([c] * (FOX_SUB // 128), axis=1)
            if diag:
                kpos = lax.broadcasted_iota(jnp.int32, s.shape, 0)
                qpos = lax.broadcasted_iota(jnp.int32, s.shape, 1) + n * FOX_SUB
                s = jnp.where(kpos <= qpos, s, -jnp.inf)
            m_prev = m_ref[:, cols]
            m_new = jnp.maximum(m_prev, jnp.max(s, axis=0, keepdims=True))
            p = jnp.exp(s - m_new)
            alpha = jnp.exp(m_prev - m_new)
            l_ref[:, cols] = alpha * l_ref[:, cols] + jnp.sum(p, axis=0, keepdims=True)
            acc_ref[:, cols] = alpha * acc_ref[:, cols] + _dot(vt_ref[0, :, :rows[n]], p.astype(BF16))
            m_ref[:, cols] = m_new

    @pl.when(j < i)
    def _():
        step(False)

    @pl.when(j == i)
    def _():
        step(True)
        o = (acc_ref[...] / l_ref[...]).T
        y_ref[0] = (o * z_ref[0].astype(F32)).astype(BF16)


def _fox_attention(q_t, k, v_t, c_rep, z_all, t):
    bsz, bw, seq = q_t.shape
    nh = bw // HEAD_DIM
    nt = seq // t
    steps = [(i, j) for i in range(nt) for j in range(i + 1)]
    qi = jnp.asarray([s[0] for s in steps], jnp.int32)
    kj = jnp.asarray([s[1] for s in steps], jnp.int32)
    grid_spec = pltpu.PrefetchScalarGridSpec(
        num_scalar_prefetch=2,
        grid=(bsz, nh, len(steps)),
        in_specs=[pl.BlockSpec((1, HEAD_DIM, t), lambda b, h, s, qi, kj: (b, h, qi[s])),
                  pl.BlockSpec((1, t, HEAD_DIM), lambda b, h, s, qi, kj: (b, kj[s], h)),
                  pl.BlockSpec((1, HEAD_DIM, t), lambda b, h, s, qi, kj: (b, h, kj[s])),
                  pl.BlockSpec((1, 1, t, 128), lambda b, h, s, qi, kj: (b, h, kj[s], 0)),
                  pl.BlockSpec((1, t, HEAD_DIM), lambda b, h, s, qi, kj: (b, qi[s], 2 * nh + h))],
        out_specs=pl.BlockSpec((1, t, HEAD_DIM), lambda b, h, s, qi, kj: (b, qi[s], h)),
        scratch_shapes=[pltpu.VMEM((1, t), F32), pltpu.VMEM((1, t), F32), pltpu.VMEM((HEAD_DIM, t), F32)],
    )
    return pl.pallas_call(
        _fox_kernel,
        out_shape=jax.ShapeDtypeStruct((bsz, seq, bw), BF16),
        grid_spec=grid_spec,
        compiler_params=_cparams(3),
        name="fox_attention",
    )(qi, kj, q_t, k, v_t, c_rep, z_all)


def _merge_kernel(h_ref, ya_ref, yb_ref, yc_ref, wg0_ref, wg1_ref, wg2_ref, wb0_ref, wb1_ref, wb2_ref, o_ref):
    h = h_ref[...]
    acc = None
    for y_ref, wg_ref, wb_ref in ((ya_ref, wg0_ref, wb0_ref), (yb_ref, wg1_ref, wb1_ref), (yc_ref, wg2_ref, wb2_ref)):
        term = _sigmoid(_dot(h, wg_ref[...])) * _dot(y_ref[...], wb_ref[0])
        acc = term if acc is None else acc + term
    o_ref[...] = acc.astype(BF16)


def _merge(h, ya, yb, yc, w_gate, w_branch, tm, tn):
    m, d = h.shape
    bw = ya.shape[1]
    nj = d // tn
    row = lambda j, i: (i, 0)
    return pl.pallas_call(
        _merge_kernel,
        out_shape=jax.ShapeDtypeStruct((m, d), BF16),
        grid=(nj, m // tm),
        in_specs=[pl.BlockSpec((tm, d), row),
                  pl.BlockSpec((tm, bw), row), pl.BlockSpec((tm, bw), row), pl.BlockSpec((tm, bw), row),
                  pl.BlockSpec((d, tn), lambda j, i: (0, j)),
                  pl.BlockSpec((d, tn), lambda j, i: (0, nj + j)),
                  pl.BlockSpec((d, tn), lambda j, i: (0, 2 * nj + j)),
                  pl.BlockSpec((1, bw, tn), lambda j, i: (0, 0, j)),
                  pl.BlockSpec((1, bw, tn), lambda j, i: (1, 0, j)),
                  pl.BlockSpec((1, bw, tn), lambda j, i: (2, 0, j))],
        out_specs=pl.BlockSpec((tm, tn), lambda j, i: (i, j)),
        compiler_params=_cparams(2),
        name="gated_merge",
    )(h, ya, yb, yc, w_gate, w_gate, w_gate, w_branch, w_branch, w_branch)


def _out_kernel(m_ref, w_ref, x_ref, pw_ref, o_ref):
    out = _dot(m_ref[...], w_ref[...])
    y = out * lax.rsqrt(jnp.mean(out * out, axis=-1, keepdims=True) + NORM_EPS)
    o_ref[...] = x_ref[...] + y * pw_ref[...]


def _out_proj(merged, w_out, x2, post_w, tm):
    m, d = x2.shape
    return pl.pallas_call(
        _out_kernel,
        out_shape=jax.ShapeDtypeStruct((m, d), F32),
        grid=(m // tm,),
        in_specs=[pl.BlockSpec((tm, d), lambda i: (i, 0)),
                  pl.BlockSpec((d, d), lambda i: (0, 0)),
                  pl.BlockSpec((tm, d), lambda i: (i, 0)),
                  pl.BlockSpec((1, d), lambda i: (0, 0))],
        out_specs=pl.BlockSpec((tm, d), lambda i: (i, 0)),
        compiler_params=_cparams(1),
        name="out_proj",
    )(merged, w_out, x2, post_w.reshape(1, d))


def _layer(x2, bsz, seq, p):
    m, d = x2.shape
    bw = d // 2
    tm = min(1024, m)
    h = _prenorm(x2, p["pre_w"], min(512, m))

    qkv_a = _proj(h, p["w_qkv_a"], epilogue="f32", tm=tm, tn=1024, out_dtype=F32)
    z_all = _proj(h, p["w_z"], epilogue="silu_bf16", tm=tm, tn=1024, out_dtype=BF16)
    u_b = _proj(h, p["w_glu"], epilogue="glu", tm=tm, tn=1024, out_dtype=F32)
    q_t = _proj_t(h, p["w_q_t"], bsz=bsz, tm=min(1024, seq), tn=1024, scale=HEAD_DIM ** -0.5)
    k_c = _proj(h, p["w_k_c"], epilogue="bf16", tm=tm, tn=1024, out_dtype=BF16)
    v_t = _proj_t(h, p["w_v_t"], bsz=bsz, tm=min(1024, seq), tn=1024)
    small_c, small_t, small_ch = _small_proj(h, p["w_small"], p["w_small_t"], p["prm_c"], p["prm_r"],
                                             min(512, seq))

    z3 = z_all.reshape(bsz, seq, 3 * bw)
    u0, w, qd, kt, qk, dec = _gdn_local(qkv_a.reshape(bsz, seq, 3 * bw), p["conv_qkv_w"], small_c, small_ch,
                                        min(256, seq))
    y_a = _gdn_scan(u0, w, qd, kt, qk, dec, z3, p["o_norm_w"], min(512, seq))

    y_b = _conformer(u_b.reshape(bsz, seq, bw), z3, p["conv_w"], p["conv_b"], p["ln_w"], p["ln_b"], min(256, seq))

    c_rep = _forget_cumsum(small_c.reshape(bsz, seq, N_SMALL), bw // HEAD_DIM, min(1024, seq))
    y_c = _fox_attention(q_t, k_c.reshape(bsz, seq, bw), v_t, c_rep, z3, min(1024, seq))

    merged = _merge(h, y_a.reshape(m, bw), y_b.reshape(m, bw), y_c.reshape(m, bw),
                    p["w_gate"], p["w_branch"], min(512, m), 512)
    return _out_proj(merged, p["w_out"], x2, p["post_w"], min(256, m))


def _prep_params(pre_norm_w, post_norm_w, w_in, conv_qkv_w, a_log, dt_bias, o_norm_w,
                 conv_w, conv_b, ln_w, ln_b, f_bias, w_branch, w_out):
    depth, d, _ = w_in.shape
    bw = d // 2
    nh = bw // HEAD_DIM
    sizes = (3 * bw, bw, nh, nh, 2 * bw, bw, 3 * bw, bw, nh, N_BRANCH * d)
    offs = [0]
    for s in sizes:
        offs.append(offs[-1] + s)
    seg = lambda k: w_in[:, :, offs[k]:offs[k + 1]]
    glu = seg(4)
    half = 512
    val = glu[:, :, :bw].reshape(depth, d, bw // half, half)
    gate = glu[:, :, bw:].reshape(depth, d, bw // half, half)
    w_glu = jnp.stack([val, gate], axis=3).reshape(depth, d, 2 * bw)
    w_small = jnp.concatenate([seg(2), seg(3), seg(8)], axis=2)
    w_small = jnp.pad(w_small, ((0, 0), (0, 0), (0, N_SMALL - 3 * nh))).astype(BF16)
    w_small_t = jnp.swapaxes(w_small[:, :, :N_SMALL_T], 1, 2)
    zeros = jnp.zeros((depth, nh), F32)
    ones = jnp.ones((depth, nh), F32)
    vec24 = lambda a, b, c: jnp.concatenate([a, b, c], axis=1)
    prm = jnp.stack([vec24(ones, jnp.exp(a_log.astype(F32)), ones),
                     vec24(zeros, dt_bias.astype(F32), zeros),
                     vec24(zeros, zeros, f_bias.astype(F32))], axis=1)
    prm_c = jnp.pad(prm, ((0, 0), (0, 8 - 3), (0, N_SMALL - 3 * nh)))
    prm_r = jnp.pad(jnp.swapaxes(prm, 1, 2), ((0, 0), (0, N_SMALL_T - 3 * nh), (0, 128 - 3)))
    return {
        "pre_w": pre_norm_w, "post_w": post_norm_w,
        "w_qkv_a": seg(0).astype(BF16),
        "w_z": jnp.concatenate([seg(1), seg(5), seg(7)], axis=2).astype(BF16),
        "w_glu": w_glu.astype(BF16),
        "w_q_t": jnp.swapaxes(seg(6)[:, :, :bw], 1, 2).astype(BF16),
        "w_k_c": seg(6)[:, :, bw:2 * bw].astype(BF16),
        "w_v_t": jnp.swapaxes(seg(6)[:, :, 2 * bw:], 1, 2).astype(BF16),
        "w_gate": seg(9).astype(BF16),
        "w_small": w_small, "w_small_t": w_small_t, "prm_c": prm_c, "prm_r": prm_r,
        "conv_qkv_w": conv_qkv_w, "o_norm_w": o_norm_w,
        "conv_w": conv_w, "conv_b": conv_b, "ln_w": ln_w, "ln_b": ln_b,
        "w_branch": w_branch.astype(BF16), "w_out": w_out.astype(BF16),
    }


def kernel(x, pre_norm_w, post_norm_w, w_in, conv_qkv_w, a_log, dt_bias, o_norm_w, conv_w, conv_b, ln_w, ln_b, f_bias, w_branch, w_out):
    bsz, seq, d = x.shape
    depth = w_in.shape[0]
    params = _prep_params(pre_norm_w, post_norm_w, w_in, conv_qkv_w, a_log, dt_bias, o_norm_w,
                          conv_w, conv_b, ln_w, ln_b, f_bias, w_branch, w_out)
    x2 = x.reshape(bsz * seq, d)
    for l in range(depth):
        x2 = _layer(x2, bsz, seq, {k: v[l] for k, v in params.items()})
    return x2.reshape(bsz, seq, d)
```

```python
import functools

import jax
import jax.numpy as jnp
from jax import lax
from jax.experimental import pallas as pl
from jax.experimental.pallas import tpu as pltpu

F32 = jnp.float32
BF16 = jnp.bfloat16

HEAD_DIM = 128
CHUNK = 64
SHORT_CONV = 4
CONF_CONV = 31
NORM_EPS = 1e-6
N_BRANCH = 3
N_SMALL = 128
N_SMALL_T = 32
V7X_VMEM_LIMIT = 48 * 1024 * 1024


def _cparams(n_axes):
    return pltpu.CompilerParams(dimension_semantics=("arbitrary",) * n_axes,
                                vmem_limit_bytes=V7X_VMEM_LIMIT)


def _sigmoid(x):
    return 1.0 / (1.0 + jnp.exp(-x))


def _silu(x):
    return x * _sigmoid(x)


def _softplus(x):
    return jnp.maximum(x, 0.0) + jnp.log(1.0 + jnp.exp(-jnp.abs(x)))


def _dot(a, b):
    return jnp.dot(a, b, preferred_element_type=F32)


def _dot_nt(a, b):
    return lax.dot_general(a, b, (((1,), (1,)), ((), ())), preferred_element_type=F32)


def _dot_tn(a, b):
    return lax.dot_general(a, b, (((0,), (0,)), ((), ())), preferred_element_type=F32)


def _dot_f32(a, b):
    return jnp.dot(a, b, preferred_element_type=F32, precision=lax.Precision.HIGHEST)


def _prenorm_kernel(x_ref, w_ref, h_ref):
    x = x_ref[...]
    y = x * lax.rsqrt(jnp.mean(x * x, axis=-1, keepdims=True) + NORM_EPS)
    h_ref[...] = (y * w_ref[...]).astype(BF16)


def _prenorm(x2, w, tm):
    m, d = x2.shape
    return pl.pallas_call(
        _prenorm_kernel,
        out_shape=jax.ShapeDtypeStruct((m, d), BF16),
        grid=(m // tm,),
        in_specs=[pl.BlockSpec((tm, d), lambda i: (i, 0)),
                  pl.BlockSpec((1, d), lambda i: (0, 0))],
        out_specs=pl.BlockSpec((tm, d), lambda i: (i, 0)),
        compiler_params=_cparams(1),
        name="prenorm",
    )(x2, w.reshape(1, d))


def _proj_kernel(h_ref, w_ref, o_ref, *, epilogue):
    acc = _dot(h_ref[...], w_ref[...])
    if epilogue == "f32":
        o_ref[...] = acc
    elif epilogue == "silu_bf16":
        o_ref[...] = _silu(acc).astype(BF16)
    elif epilogue == "glu":
        half = acc.shape[1] // 2
        o_ref[...] = acc[:, :half] * _sigmoid(acc[:, half:])
    elif epilogue == "bf16":
        o_ref[...] = acc.astype(BF16)
    else:
        raise ValueError(epilogue)


def _proj_t_kernel(wt_ref, h_ref, o_ref, *, scale):
    o_ref[0] = (_dot_nt(wt_ref[...], h_ref[...]) * scale).astype(BF16)


def _proj_t(h, w_t, *, bsz, tm, tn, scale=1.0):
    m, k = h.shape
    n = w_t.shape[0]
    seq = m // bsz
    nts = seq // tm
    return pl.pallas_call(
        functools.partial(_proj_t_kernel, scale=scale),
        out_shape=jax.ShapeDtypeStruct((bsz, n, seq), BF16),
        grid=(n // tn, m // tm),
        in_specs=[pl.BlockSpec((tn, k), lambda j, i: (j, 0)),
                  pl.BlockSpec((tm, k), lambda j, i: (i, 0))],
        out_specs=pl.BlockSpec((1, tn, tm), lambda j, i: (i // nts, j, i % nts)),
        compiler_params=_cparams(2),
        name="proj_t",
    )(w_t, h)


def _proj(h, w, *, epilogue, tm, tn, out_dtype):
    m, k = h.shape
    n = w.shape[1]
    tn_out = tn // 2 if epilogue == "glu" else tn
    n_out = n // 2 if epilogue == "glu" else n
    return pl.pallas_call(
        functools.partial(_proj_kernel, epilogue=epilogue),
        out_shape=jax.ShapeDtypeStruct((m, n_out), out_dtype),
        grid=(n // tn, m // tm),
        in_specs=[pl.BlockSpec((tm, k), lambda j, i: (i, 0)),
                  pl.BlockSpec((k, tn), lambda j, i: (0, j))],
        out_specs=pl.BlockSpec((tm, tn_out), lambda j, i: (i, j)),
        compiler_params=_cparams(2),
        name="proj_" + epilogue,
    )(h, w)


def _gate_transform(x, exp_a, dt_bias, f_bias, idx):
    beta = _sigmoid(x)
    log_a = -exp_a * _softplus(x + dt_bias)
    log_f = -_softplus(-(x + f_bias))
    return jnp.where(idx < 8, beta, jnp.where(idx < 16, log_a, log_f))


def _small_kernel(h_ref, w_ref, wt_ref, pc_ref, pr_ref, o_ref, ot_ref, oc_ref):
    h = h_ref[...]
    col = _dot(h, w_ref[...])
    row = _dot_nt(wt_ref[...], h)
    pc = pc_ref[...]
    cidx = lax.broadcasted_iota(jnp.int32, col.shape, 1)
    o_ref[...] = _gate_transform(col, pc[0:1, :], pc[1:2, :], pc[2:3, :], cidx)
    pr = pr_ref[...]
    ridx = lax.broadcasted_iota(jnp.int32, row.shape, 0)
    row = _gate_transform(row, pr[:, 0:1], pr[:, 1:2], pr[:, 2:3], ridx)
    ot_ref[...] = row
    for c in range(row.shape[1] // CHUNK):
        oc_ref[c] = row[:, c * CHUNK:(c + 1) * CHUNK]


def _small_proj(h, w_small, w_small_t, prm_c, prm_r, tm):
    m, k = h.shape
    return pl.pallas_call(
        _small_kernel,
        out_shape=(jax.ShapeDtypeStruct((m, N_SMALL), F32),
                   jax.ShapeDtypeStruct((N_SMALL_T, m), F32),
                   jax.ShapeDtypeStruct((m // CHUNK, N_SMALL_T, CHUNK), F32)),
        grid=(m // tm,),
        in_specs=[pl.BlockSpec((tm, k), lambda i: (i, 0)),
                  pl.BlockSpec((k, N_SMALL), lambda i: (0, 0)),
                  pl.BlockSpec((N_SMALL_T, k), lambda i: (0, 0)),
                  pl.BlockSpec((8, N_SMALL), lambda i: (0, 0)),
                  pl.BlockSpec((N_SMALL_T, 128), lambda i: (0, 0))],
        out_specs=(pl.BlockSpec((tm, N_SMALL), lambda i: (i, 0)),
                   pl.BlockSpec((N_SMALL_T, tm), lambda i: (0, i)),
                   pl.BlockSpec((tm // CHUNK, N_SMALL_T, CHUNK), lambda i: (i, 0, 0))),
        compiler_params=_cparams(1),
        name="small_proj",
    )(h, w_small, w_small_t, prm_c, prm_r)


def _split3(x):
    a = x.astype(BF16)
    r1 = x - a.astype(F32)
    b = r1.astype(BF16)
    c = (r1 - b.astype(F32)).astype(BF16)
    return a, b, c


def _cumsum_kernel(x_ref, o_ref, carry_ref):
    ts = x_ref.shape[1]
    nh = o_ref.shape[1]
    blk = 128

    @pl.when(pl.program_id(1) == 0)
    def _():
        carry_ref[...] = jnp.zeros_like(carry_ref)

    r = lax.broadcasted_iota(jnp.int32, (blk, blk), 0)
    c = lax.broadcasted_iota(jnp.int32, (blk, blk), 1)
    tril = jnp.where(r >= c, 1.0, 0.0).astype(BF16)
    er = lax.broadcasted_iota(jnp.int32, (blk, nh * blk), 0)
    ec = lax.broadcasted_iota(jnp.int32, (blk, nh * blk), 1)
    spread = jnp.where(er == 2 * nh + ec // blk, 1.0, 0.0).astype(BF16)

    carry = carry_ref[0:1, :]
    for j in range(ts // blk):
        rows = slice(j * blk, (j + 1) * blk)
        x1, x2, x3 = _split3(x_ref[0, rows, :])
        run = _dot(tril, x1) + _dot(tril, x2) + _dot(tril, x3) + carry
        carry = run[blk - 1:blk, :]
        c1, c2, c3 = _split3(run)
        rep = _dot(c1, spread) + _dot(c2, spread) + _dot(c3, spread)
        for h in range(nh):
            o_ref[0, h, rows, :] = rep[:, h * blk:(h + 1) * blk]
    carry_ref[0:1, :] = carry


def _forget_cumsum(small_c3, nh, ts):
    bsz, seq, _ = small_c3.shape
    return pl.pallas_call(
        _cumsum_kernel,
        out_shape=jax.ShapeDtypeStruct((bsz, nh, seq, 128), F32),
        grid=(bsz, seq // ts),
        in_specs=[pl.BlockSpec((1, ts, N_SMALL), lambda b, i: (b, i, 0))],
        out_specs=pl.BlockSpec((1, nh, ts, 128), lambda b, i: (b, 0, i, 0)),
        scratch_shapes=[pltpu.VMEM((8, N_SMALL), F32)],
        compiler_params=_cparams(2),
        name="forget_cumsum",
    )(small_c3)


GDN_CHUNKS_PER_ITER = 2


def _gdn_local_kernel(qkv_ref, halo_ref, cw_ref, sc_ref, sr_ref,
                      u0_ref, w_ref, qd_ref, kt_ref, qk_ref, dec_ref, xx_ref, act_ref):
    i = pl.program_id(1)
    ts = qkv_ref.shape[1]
    bw = u0_ref.shape[2]
    nh = bw // HEAD_DIM
    halo = halo_ref[0]
    xx_ref[0:8, :] = jnp.where(i > 0, halo, 0.0)
    xx_ref[8:, :] = qkv_ref[0]

    cw = cw_ref[...]
    for rb in range(ts // CHUNK):
        acc = None
        for k in range(SHORT_CONV):
            start = rb * CHUNK + (8 - (SHORT_CONV - 1)) + k
            term = cw[k:k + 1, :] * xx_ref[start:start + CHUNK, :]
            acc = term if acc is None else acc + term
        act_ref[rb * CHUNK:(rb + 1) * CHUNK, :] = _silu(acc)

    r = lax.broadcasted_iota(jnp.int32, (CHUNK, CHUNK), 0)
    c = lax.broadcasted_iota(jnp.int32, (CHUNK, CHUNK), 1)
    incl = r >= c
    strict = r > c
    tril = incl.astype(F32)
    triu = (r <= c).astype(F32)
    eye = jnp.where(r == c, 1.0, 0.0).astype(F32)
    nchunks = ts // CHUNK
    cpi = GDN_CHUNKS_PER_ITER if nchunks % GDN_CHUNKS_PER_ITER == 0 else 1

    def iter_body(it, _):
        units = []
        for cc in range(cpi):
            ci = it * cpi + cc
            base = pl.multiple_of(ci * CHUNK, CHUNK)
            rows = pl.ds(base, CHUNK)
            sc = sc_ref[0, rows, :]
            g_col = _dot_f32(tril, sc)
            g_row = _dot_f32(sr_ref[ci], triu)
            for h in range(nh):
                q = act_ref[rows, h * HEAD_DIM:(h + 1) * HEAD_DIM]
                k_ = act_ref[rows, bw + h * HEAD_DIM: bw + (h + 1) * HEAD_DIM]
                v = act_ref[rows, 2 * bw + h * HEAD_DIM: 2 * bw + (h + 1) * HEAD_DIM]
                q = q * (lax.rsqrt(jnp.sum(q * q, axis=-1, keepdims=True) + NORM_EPS) * HEAD_DIM ** -0.5)
                k_ = k_ * lax.rsqrt(jnp.sum(k_ * k_, axis=-1, keepdims=True) + NORM_EPS)
                beta = sc[:, h:h + 1]
                g = g_col[:, nh + h:nh + h + 1]
                g_last = g_col[CHUNK - 1:CHUNK, nh + h:nh + h + 1]
                gr = g_row[nh + h:nh + h + 1, :]
                units.append(dict(
                    ci=ci, rows=rows, h=h, q=q, k=k_, v=v, beta=beta, g=g, g_last=g_last,
                    decay=jnp.exp(jnp.where(incl, g - gr, -jnp.inf)),
                    eg=jnp.exp(g), kb=k_.astype(BF16), qb=q.astype(BF16)))

        for u in units:
            u["kq"] = _dot_nt(jnp.concatenate([u["kb"], u["qb"]], axis=0), u["kb"])
        for u in units:
            u["qk"] = u["kq"][CHUNK:] * u["decay"]
            a = jnp.where(strict, -(u["beta"] * u["kq"][:CHUNK] * u["decay"]), 0.0)
            u["x"] = eye + a
            u["p"] = a
        for u in units:
            pb = u["p"].astype(BF16)
            u["p"] = _dot(pb, pb)
        m = 2
        while 2 * m < CHUNK:
            for u in units:
                pb = u["p"].astype(BF16)
                out = _dot(jnp.concatenate([u["x"].astype(BF16), pb], axis=0), pb)
                u["x"] = u["x"] + out[:CHUNK]
                u["p"] = out[CHUNK:]
            m *= 2
        for u in units:
            u["x"] = u["x"] + _dot(u["x"].astype(BF16), u["p"].astype(BF16))
        for u in units:
            rhs = jnp.concatenate([u["v"] * u["beta"], u["k"] * (u["beta"] * u["eg"])], axis=1)
            u["uw"] = _dot(u["x"].astype(BF16), rhs.astype(BF16))

        for u in units:
            h, rows = u["h"], u["rows"]
            sl = slice(h * HEAD_DIM, (h + 1) * HEAD_DIM)
            u0_ref[0, rows, sl] = u["uw"][:, :HEAD_DIM]
            w_ref[0, rows, sl] = u["uw"][:, HEAD_DIM:].astype(BF16)
            qd_ref[0, rows, sl] = (u["q"] * u["eg"]).astype(BF16)
            kt_ref[0, rows, sl] = (u["k"] * jnp.exp(u["g_last"] - u["g"])).astype(BF16)
            qk_ref[0, rows, h * CHUNK:(h + 1) * CHUNK] = u["qk"].astype(BF16)
            dec_ref[0, u["ci"], :, sl] = jnp.broadcast_to(jnp.exp(u["g_last"]), (1, HEAD_DIM))
        return 0

    lax.fori_loop(0, nchunks // cpi, iter_body, 0)


def _gdn_local(qkv_a, conv_w, small_c, small_ch, ts):
    bsz, seq, n3 = qkv_a.shape
    bw = n3 // 3
    nh = bw // HEAD_DIM
    nt = seq // ts
    nc = ts // CHUNK
    big = lambda b, i: (b, i, 0)
    return pl.pallas_call(
        _gdn_local_kernel,
        out_shape=(jax.ShapeDtypeStruct((bsz, seq, bw), F32),
                   jax.ShapeDtypeStruct((bsz, seq, bw), BF16),
                   jax.ShapeDtypeStruct((bsz, seq, bw), BF16),
                   jax.ShapeDtypeStruct((bsz, seq, bw), BF16),
                   jax.ShapeDtypeStruct((bsz, seq, nh * CHUNK), BF16),
                   jax.ShapeDtypeStruct((bsz, seq // CHUNK, 1, bw), F32)),
        grid=(bsz, nt),
        in_specs=[pl.BlockSpec((1, ts, n3), big),
                  pl.BlockSpec((1, 8, n3), lambda b, i: (b, jnp.maximum(i * (ts // 8) - 1, 0), 0)),
                  pl.BlockSpec((SHORT_CONV, n3), lambda b, i: (0, 0)),
                  pl.BlockSpec((1, ts, N_SMALL), big),
                  pl.BlockSpec((nc, N_SMALL_T, CHUNK), lambda b, i: (b * nt + i, 0, 0))],
        out_specs=(pl.BlockSpec((1, ts, bw), big),
                   pl.BlockSpec((1, ts, bw), big),
                   pl.BlockSpec((1, ts, bw), big),
                   pl.BlockSpec((1, ts, bw), big),
                   pl.BlockSpec((1, ts, nh * CHUNK), big),
                   pl.BlockSpec((1, nc, 1, bw), lambda b, i: (b, i, 0, 0))),
        scratch_shapes=[pltpu.VMEM((ts + 8, n3), F32), pltpu.VMEM((ts, n3), F32)],
        compiler_params=_cparams(2),
        name="gdn_local",
    )(qkv_a, qkv_a, conv_w, small_c.reshape(bsz, seq, N_SMALL), small_ch)


def _gdn_scan_kernel(u0_ref, w_ref, qd_ref, kt_ref, qk_ref, dec_ref, z_ref, nw_ref, y_ref, state_ref):
    i = pl.program_id(0)
    bsz, ts, bw = u0_ref.shape
    nh = bw // HEAD_DIM

    @pl.when(i == 0)
    def _():
        state_ref[...] = jnp.zeros_like(state_ref)

    nw = nw_ref[...]

    def chunk_body(ci, _):
        rows = pl.ds(pl.multiple_of(ci * CHUNK, CHUNK), CHUNK)
        units = []
        for b in range(bsz):
            for h in range(nh):
                sl = slice(h * HEAD_DIM, (h + 1) * HEAD_DIM)
                state = state_ref[b * nh + h]
                lhs = jnp.concatenate([w_ref[b, rows, sl], qd_ref[b, rows, sl]], axis=0)
                units.append(dict(b=b, h=h, sl=sl, state=state, ws=_dot(lhs, state.astype(BF16))))
        for u in units:
            b, h, sl = u["b"], u["h"], u["sl"]
            ub = (u0_ref[b, rows, sl] - u["ws"][:CHUNK]).astype(BF16)
            u["o"] = u["ws"][CHUNK:] + _dot(qk_ref[b, rows, h * CHUNK:(h + 1) * CHUNK], ub)
            u["ds"] = _dot_tn(kt_ref[b, rows, sl], ub)
        for u in units:
            b, h, sl = u["b"], u["h"], u["sl"]
            state_ref[b * nh + h] = u["state"] * dec_ref[b, ci, :, sl] + u["ds"]
            o = u["o"]
            o = o * lax.rsqrt(jnp.mean(o * o, axis=-1, keepdims=True) + NORM_EPS)
            y_ref[b, rows, sl] = (o * nw * z_ref[b, rows, sl].astype(F32)).astype(BF16)
        return 0

    lax.fori_loop(0, ts // CHUNK, chunk_body, 0)


def _gdn_scan(u0, w, qd, kt, qk, dec, z_all, o_norm_w, ts):
    bsz, seq, bw = u0.shape
    nh = bw // HEAD_DIM
    big = lambda i: (0, i, 0)
    return pl.pallas_call(
        _gdn_scan_kernel,
        out_shape=jax.ShapeDtypeStruct((bsz, seq, bw), BF16),
        grid=(seq // ts,),
        in_specs=[pl.BlockSpec((bsz, ts, bw), big),
                  pl.BlockSpec((bsz, ts, bw), big),
                  pl.BlockSpec((bsz, ts, bw), big),
                  pl.BlockSpec((bsz, ts, bw), big),
                  pl.BlockSpec((bsz, ts, nh * CHUNK), big),
                  pl.BlockSpec((bsz, ts // CHUNK, 1, bw), lambda i: (0, i, 0, 0)),
                  pl.BlockSpec((bsz, ts, bw), big),
                  pl.BlockSpec((1, HEAD_DIM), lambda i: (0, 0))],
        out_specs=pl.BlockSpec((bsz, ts, bw), big),
        scratch_shapes=[pltpu.VMEM((bsz * nh, HEAD_DIM, HEAD_DIM), F32)],
        compiler_params=_cparams(1),
        name="gdn_scan",
    )(u0, w, qd, kt, qk, dec, z_all, o_norm_w.reshape(1, HEAD_DIM))


def _conformer_kernel(u_ref, halo_ref, z_ref, cw_ref, cb_ref, lw_ref, lb_ref, y_ref, xx_ref):
    i = pl.program_id(1)
    ts = u_ref.shape[1]
    pad = halo_ref.shape[1]
    xx_ref[0:pad, :] = jnp.where(i > 0, halo_ref[0], 0.0)
    xx_ref[pad:, :] = u_ref[0]
    cw = cw_ref[...]
    off = pad - (CONF_CONV - 1)
    acc = None
    for r in range(8):
        rows = ts if r == 0 else ts + 8
        y = None
        for m in range((pad + 8) // 8):
            k = 8 * m + r - off
            if 0 <= k < CONF_CONV:
                term = cw[k:k + 1, :] * xx_ref[8 * m:8 * m + rows, :]
                y = term if y is None else y + term
        shifted = y[r:r + ts, :]
        acc = shifted if acc is None else acc + shifted
    u = acc + cb_ref[...]
    xc = u - jnp.mean(u, axis=-1, keepdims=True)
    y = xc * lax.rsqrt(jnp.mean(xc * xc, axis=-1, keepdims=True) + NORM_EPS)
    y = _silu(y * lw_ref[...] + lb_ref[...])
    y_ref[0] = (y * z_ref[0].astype(F32)).astype(BF16)


def _conformer(u, z_all, conv_w, conv_b, ln_w, ln_b, ts):
    bsz, seq, bw = u.shape
    pad = 32
    big = lambda b, i: (b, i, 0)
    vec = lambda b, i: (0, 0)
    return pl.pallas_call(
        _conformer_kernel,
        out_shape=jax.ShapeDtypeStruct((bsz, seq, bw), BF16),
        grid=(bsz, seq // ts),
        in_specs=[pl.BlockSpec((1, ts, bw), big),
                  pl.BlockSpec((1, pad, bw), lambda b, i: (b, jnp.maximum(i * (ts // pad) - 1, 0), 0)),
                  pl.BlockSpec((1, ts, bw), lambda b, i: (b, i, 1)),
                  pl.BlockSpec((CONF_CONV, bw), vec),
                  pl.BlockSpec((1, bw), vec),
                  pl.BlockSpec((1, bw), vec),
                  pl.BlockSpec((1, bw), vec)],
        out_specs=pl.BlockSpec((1, ts, bw), big),
        scratch_shapes=[pltpu.VMEM((ts + pad, bw), F32)],
        compiler_params=_cparams(2),
        name="conformer",
    )(u, u, z_all, conv_w, conv_b.reshape(1, bw), ln_w.reshape(1, bw), ln_b.reshape(1, bw))


FOX_SUB = 256


def _fox_kernel(qi_ref, kj_ref, qt_ref, k_ref, vt_ref, c_ref, z_ref, y_ref, m_ref, l_ref, acc_ref):
    t = pl.program_id(2)
    i = qi_ref[t]
    j = kj_ref[t]
    tq = qt_ref.shape[2]
    tk = k_ref.shape[1]
    nsub = tq // FOX_SUB

    @pl.when(j == 0)
    def _():
        m_ref[...] = jnp.full_like(m_ref, -jnp.inf)
        l_ref[...] = jnp.zeros_like(l_ref)
        acc_ref[...] = jnp.zeros_like(acc_ref)

    def step(diag):
        rows = [min((n + 1) * FOX_SUB, tk) if diag else tk for n in range(nsub)]
        s_all = [_dot(k_ref[0, :rows[n], :], qt_ref[0, :, n * FOX_SUB:(n + 1) * FOX_SUB]) for n in range(nsub)]
        for n in range(nsub):
            cols = slice(n * FOX_SUB, (n + 1) * FOX_SUB)
            c = c_ref[0, 0, :rows[n], :]
            s = s_all[n] - jnp.concatenate([c] * (FOX_SUB // 128), axis=1)
            if diag:
                kpos = lax.broadcasted_iota(jnp.int32, s.shape, 0)
                qpos = lax.broadcasted_iota(jnp.int32, s.shape, 1) + n * FOX_SUB
                s = jnp.where(kpos <= qpos, s, -jnp.inf)
            m_prev = m_ref[:, cols]
            m_new = jnp.maximum(m_prev, jnp.max(s, axis=0, keepdims=True))
            p = jnp.exp(s - m_new)
            alpha = jnp.exp(m_prev - m_new)
            l_ref[:, cols] = alpha * l_ref[:, cols] + jnp.sum(p, axis=0, keepdims=True)
            acc_ref[:, cols] = alpha * acc_ref[:, cols] + _dot(vt_ref[0, :, :rows[n]], p.astype(BF16))
            m_ref[:, cols] = m_new

    @pl.when(j < i)
    def _():
        step(False)

    @pl.when(j == i)
    def _():
        step(True)
        o = (acc_ref[...] / l_ref[...]).T
        y_ref[0] = (o * z_ref[0].astype(F32)).astype(BF16)


def _fox_attention(q_t, k, v_t, c_rep, z_all, t):
    bsz, bw, seq = q_t.shape
    nh = bw // HEAD_DIM
    nt = seq // t
    steps = [(i, j) for i in range(nt) for j in range(i + 1)]
    qi = jnp.asarray([s[0] for s in steps], jnp.int32)
    kj = jnp.asarray([s[1] for s in steps], jnp.int32)
    grid_spec = pltpu.PrefetchScalarGridSpec(
        num_scalar_prefetch=2,
        grid=(bsz, nh, len(steps)),
        in_specs=[pl.BlockSpec((1, HEAD_DIM, t), lambda b, h, s, qi, kj: (b, h, qi[s])),
                  pl.BlockSpec((1, t, HEAD_DIM), lambda b, h, s, qi, kj: (b, kj[s], h)),
                  pl.BlockSpec((1, HEAD_DIM, t), lambda b, h, s, qi, kj: (b, h, kj[s])),
                  pl.BlockSpec((1, 1, t, 128), lambda b, h, s, qi, kj: (b, h, kj[s], 0)),
                  pl.BlockSpec((1, t, HEAD_DIM), lambda b, h, s, qi, kj: (b, qi[s], 2 * nh + h))],
        out_specs=pl.BlockSpec((1, t, HEAD_DIM), lambda b, h, s, qi, kj: (b, qi[s], h)),
        scratch_shapes=[pltpu.VMEM((1, t), F32), pltpu.VMEM((1, t), F32), pltpu.VMEM((HEAD_DIM, t), F32)],
    )
    return pl.pallas_call(
        _fox_kernel,
        out_shape=jax.ShapeDtypeStruct((bsz, seq, bw), BF16),
        grid_spec=grid_spec,
        compiler_params=_cparams(3),
        name="fox_attention",
    )(qi, kj, q_t, k, v_t, c_rep, z_all)


def _merge_kernel(h_ref, ya_ref, yb_ref, yc_ref, wg0_ref, wg1_ref, wg2_ref, wb0_ref, wb1_ref, wb2_ref, o_ref):
    h = h_ref[...]
    acc = None
    for y_ref, wg_ref, wb_ref in ((ya_ref, wg0_ref, wb0_ref), (yb_ref, wg1_ref, wb1_ref), (yc_ref, wg2_ref, wb2_ref)):
        term = _sigmoid(_dot(h, wg_ref[...])) * _dot(y_ref[...], wb_ref[0])
        acc = term if acc is None else acc + term
    o_ref[...] = acc.astype(BF16)


def _merge(h, ya, yb, yc, w_gate, w_branch, tm, tn):
    m, d = h.shape
    bw = ya.shape[1]
    nj = d // tn
    row = lambda j, i: (i, 0)
    return pl.pallas_call(
        _merge_kernel,
        out_shape=jax.ShapeDtypeStruct((m, d), BF16),
        grid=(nj, m // tm),
        in_specs=[pl.BlockSpec((tm, d), row),
                  pl.BlockSpec((tm, bw), row), pl.BlockSpec((tm, bw), row), pl.BlockSpec((tm, bw), row),
                  pl.BlockSpec((d, tn), lambda j, i: (0, j)),
                  pl.BlockSpec((d, tn), lambda j, i: (0, nj + j)),
                  pl.BlockSpec((d, tn), lambda j, i: (0, 2 * nj + j)),
                  pl.BlockSpec((1, bw, tn), lambda j, i: (0, 0, j)),
                  pl.BlockSpec((1, bw, tn), lambda j, i: (1, 0, j)),
                  pl.BlockSpec((1, bw, tn), lambda j, i: (2, 0, j))],
        out_specs=pl.BlockSpec((tm, tn), lambda j, i: (i, j)),
        compiler_params=_cparams(2),
        name="gated_merge",
    )(h, ya, yb, yc, w_gate, w_gate, w_gate, w_branch, w_branch, w_branch)


def _out_kernel(m_ref, w_ref, x_ref, pw_ref, o_ref):
    out = _dot(m_ref[...], w_ref[...])
    y = out * lax.rsqrt(jnp.mean(out * out, axis=-1, keepdims=True) + NORM_EPS)
    o_ref[...] = x_ref[...] + y * pw_ref[...]


def _out_proj(merged, w_out, x2, post_w, tm):
    m, d = x2.shape
    return pl.pallas_call(
        _out_kernel,
        out_shape=jax.ShapeDtypeStruct((m, d), F32),
        grid=(m // tm,),
        in_specs=[pl.BlockSpec((tm, d), lambda i: (i, 0)),
                  pl.BlockSpec((d, d), lambda i: (0, 0)),
                  pl.BlockSpec((tm, d), lambda i: (i, 0)),
                  pl.BlockSpec((1, d), lambda i: (0, 0))],
        out_specs=pl.BlockSpec((tm, d), lambda i: (i, 0)),
        compiler_params=_cparams(1),
        name="out_proj",
    )(merged, w_out, x2, post_w.reshape(1, d))


def _layer(x2, bsz, seq, p):
    m, d = x2.shape
    bw = d // 2
    tm = min(1024, m)
    h = _prenorm(x2, p["pre_w"], min(512, m))

    qkv_a = _proj(h, p["w_qkv_a"], epilogue="f32", tm=tm, tn=1024, out_dtype=F32)
    z_all = _proj(h, p["w_z"], epilogue="silu_bf16", tm=tm, tn=1024, out_dtype=BF16)
    u_b = _proj(h, p["w_glu"], epilogue="glu", tm=tm, tn=1024, out_dtype=F32)
    q_t = _proj_t(h, p["w_q_t"], bsz=bsz, tm=min(1024, seq), tn=1024, scale=HEAD_DIM ** -0.5)
    k_c = _proj(h, p["w_k_c"], epilogue="bf16", tm=tm, tn=1024, out_dtype=BF16)
    v_t = _proj_t(h, p["w_v_t"], bsz=bsz, tm=min(1024, seq), tn=1024)
    small_c, small_t, small_ch = _small_proj(h, p["w_small"], p["w_small_t"], p["prm_c"], p["prm_r"],
                                             min(512, seq))

    z3 = z_all.reshape(bsz, seq, 3 * bw)
    u0, w, qd, kt, qk, dec = _gdn_local(qkv_a.reshape(bsz, seq, 3 * bw), p["conv_qkv_w"], small_c, small_ch,
                                        min(256, seq))
    y_a = _gdn_scan(u0, w, qd, kt, qk, dec, z3, p["o_norm_w"], min(512, seq))

    y_b = _conformer(u_b.reshape(bsz, seq, bw), z3, p["conv_w"], p["conv_b"], p["ln_w"], p["ln_b"], min(256, seq))

    c_rep = _forget_cumsum(small_c.reshape(bsz, seq, N_SMALL), bw // HEAD_DIM, min(1024, seq))
    y_c = _fox_attention(q_t, k_c.reshape(bsz, seq, bw), v_t, c_rep, z3, min(1024, seq))

    merged = _merge(h, y_a.reshape(m, bw), y_b.reshape(m, bw), y_c.reshape(m, bw),
                    p["w_gate"], p["w_branch"], min(512, m), 512)
    return _out_proj(merged, p["w_out"], x2, p["post_w"], min(256, m))


def _prep_params(pre_norm_w, post_norm_w, w_in, conv_qkv_w, a_log, dt_bias, o_norm_w,
                 conv_w, conv_b, ln_w, ln_b, f_bias, w_branch, w_out):
    depth, d, _ = w_in.shape
    bw = d // 2
    nh = bw // HEAD_DIM
    sizes = (3 * bw, bw, nh, nh, 2 * bw, bw, 3 * bw, bw, nh, N_BRANCH * d)
    offs = [0]
    for s in sizes:
        offs.append(offs[-1] + s)
    seg = lambda k: w_in[:, :, offs[k]:offs[k + 1]]
    glu = seg(4)
    half = 512
    val = glu[:, :, :bw].reshape(depth, d, bw // half, half)
    gate = glu[:, :, bw:].reshape(depth, d, bw // half, half)
    w_glu = jnp.stack([val, gate], axis=3).reshape(depth, d, 2 * bw)
    w_small = jnp.concatenate([seg(2), seg(3), seg(8)], axis=2)
    w_small = jnp.pad(w_small, ((0, 0), (0, 0), (0, N_SMALL - 3 * nh))).astype(BF16)
    w_small_t = jnp.swapaxes(w_small[:, :, :N_SMALL_T], 1, 2)
    zeros = jnp.zeros((depth, nh), F32)
    ones = jnp.ones((depth, nh), F32)
    vec24 = lambda a, b, c: jnp.concatenate([a, b, c], axis=1)
    prm = jnp.stack([vec24(ones, jnp.exp(a_log.astype(F32)), ones),
                     vec24(zeros, dt_bias.astype(F32), zeros),
                     vec24(zeros, zeros, f_bias.astype(F32))], axis=1)
    prm_c = jnp.pad(prm, ((0, 0), (0, 8 - 3), (0, N_SMALL - 3 * nh)))
    prm_r = jnp.pad(jnp.swapaxes(prm, 1, 2), ((0, 0), (0, N_SMALL_T - 3 * nh), (0, 128 - 3)))
    return {
        "pre_w": pre_norm_w, "post_w": post_norm_w,
        "w_qkv_a": seg(0).astype(BF16),
        "w_z": jnp.concatenate([seg(1), seg(5), seg(7)], axis=2).astype(BF16),
        "w_glu": w_glu.astype(BF16),
        "w_q_t": jnp.swapaxes(seg(6)[:, :, :bw], 1, 2).astype(BF16),
        "w_k_c": seg(6)[:, :, bw:2 * bw].astype(BF16),
        "w_v_t": jnp.swapaxes(seg(6)[:, :, 2 * bw:], 1, 2).astype(BF16),
        "w_gate": seg(9).astype(BF16),
        "w_small": w_small, "w_small_t": w_small_t, "prm_c": prm_c, "prm_r": prm_r,
        "conv_qkv_w": conv_qkv_w, "o_norm_w": o_norm_w,
        "conv_w": conv_w, "conv_b": conv_b, "ln_w": ln_w, "ln_b": ln_b,
        "w_branch": w_branch.astype(BF16), "w_out": w_out.astype(BF16),
    }


def kernel(x, pre_norm_w, post_norm_w, w_in, conv_qkv_w, a_log, dt_bias, o_norm_w, conv_w, conv_b, ln_w, ln_b, f_bias, w_branch, w_out):
    bsz, seq, d = x.shape
    depth = w_in.shape[0]
    params = _prep_params(pre_norm_w, post_norm_w, w_in, conv_qkv_w, a_log, dt_bias, o_norm_w,
                          conv_w, conv_b, ln_w, ln_b, f_bias, w_branch, w_out)
    x2 = x.reshape(bsz * seq, d)
    for l in range(depth):
        x2 = _layer(x2, bsz, seq, {k: v[l] for k, v in params.items()})
    return x2.reshape(bsz, seq, d)
```

```python
import functools

import jax
import jax.numpy as jnp
from jax import lax
from jax.experimental import pallas as pl
from jax.experimental.pallas import tpu as pltpu

F32 = jnp.float32
BF16 = jnp.bfloat16

HEAD_DIM = 128
CHUNK = 64
SHORT_CONV = 4
CONF_CONV = 31
NORM_EPS = 1e-6
N_BRANCH = 3
N_SMALL = 128
N_SMALL_T = 32
V7X_VMEM_LIMIT = 48 * 1024 * 1024


def _cparams(n_axes):
    return pltpu.CompilerParams(dimension_semantics=("arbitrary",) * n_axes,
                                vmem_limit_bytes=V7X_VMEM_LIMIT)


def _sigmoid(x):
    return 1.0 / (1.0 + jnp.exp(-x))


def _silu(x):
    return x * _sigmoid(x)


def _softplus(x):
    return jnp.maximum(x, 0.0) + jnp.log(1.0 + jnp.exp(-jnp.abs(x)))


def _dot(a, b):
    return jnp.dot(a, b, preferred_element_type=F32)


def _dot_nt(a, b):
    return lax.dot_general(a, b, (((1,), (1,)), ((), ())), preferred_element_type=F32)


def _dot_tn(a, b):
    return lax.dot_general(a, b, (((0,), (0,)), ((), ())), preferred_element_type=F32)


def _dot_f32(a, b):
    return jnp.dot(a, b, preferred_element_type=F32, precision=lax.Precision.HIGHEST)


def _prenorm_kernel(x_ref, w_ref, h_ref):
    x = x_ref[...]
    y = x * lax.rsqrt(jnp.mean(x * x, axis=-1, keepdims=True) + NORM_EPS)
    h_ref[...] = (y * w_ref[...]).astype(BF16)


def _prenorm(x2, w, tm):
    m, d = x2.shape
    return pl.pallas_call(
        _prenorm_kernel,
        out_shape=jax.ShapeDtypeStruct((m, d), BF16),
        grid=(m // tm,),
        in_specs=[pl.BlockSpec((tm, d), lambda i: (i, 0)),
                  pl.BlockSpec((1, d), lambda i: (0, 0))],
        out_specs=pl.BlockSpec((tm, d), lambda i: (i, 0)),
        compiler_params=_cparams(1),
        name="prenorm",
    )(x2, w.reshape(1, d))


def _proj_kernel(h_ref, w_ref, o_ref, *, epilogue):
    acc = _dot(h_ref[...], w_ref[...])
    if epilogue == "f32":
        o_ref[...] = acc
    elif epilogue == "silu_bf16":
        o_ref[...] = _silu(acc).astype(BF16)
    elif epilogue == "bf16":
        o_ref[...] = acc.astype(BF16)
    else:
        raise ValueError(epilogue)


def _proj(h, w_packed, col0, n, *, epilogue, tm, tn, out_dtype):
    m, k = h.shape
    return pl.pallas_call(
        functools.partial(_proj_kernel, epilogue=epilogue),
        out_shape=jax.ShapeDtypeStruct((m, n), out_dtype),
        grid=(n // tn, m // tm),
        in_specs=[pl.BlockSpec((tm, k), lambda j, i: (i, 0)),
                  pl.BlockSpec((k, tn), lambda j, i: (0, col0 // tn + j))],
        out_specs=pl.BlockSpec((tm, tn), lambda j, i: (i, j)),
        compiler_params=_cparams(2),
        name="proj_" + epilogue,
    )(h, w_packed)


def _proj_t_kernel(wt_ref, h_ref, o_ref, *, scale):
    o_ref[0] = (_dot_nt(wt_ref[...], h_ref[...]) * scale).astype(BF16)


def _proj_t(h, w_t, *, bsz, tm, tn, scale=1.0):
    m, k = h.shape
    n = w_t.shape[0]
    seq = m // bsz
    nts = seq // tm
    return pl.pallas_call(
        functools.partial(_proj_t_kernel, scale=scale),
        out_shape=jax.ShapeDtypeStruct((bsz, n, seq), BF16),
        grid=(n // tn, m // tm),
        in_specs=[pl.BlockSpec((tn, k), lambda j, i: (j, 0)),
                  pl.BlockSpec((tm, k), lambda j, i: (i, 0))],
        out_specs=pl.BlockSpec((1, tn, tm), lambda j, i: (i // nts, j, i % nts)),
        compiler_params=_cparams(2),
        name="proj_t",
    )(w_t, h)


def _glu_kernel(h_ref, wv_ref, wg_ref, o_ref):
    h = h_ref[...]
    o_ref[...] = _dot(h, wv_ref[...]) * _sigmoid(_dot(h, wg_ref[...]))


def _proj_glu(h, w_packed, col0, n, *, tm, tn):
    m, k = h.shape
    return pl.pallas_call(
        _glu_kernel,
        out_shape=jax.ShapeDtypeStruct((m, n), F32),
        grid=(n // tn, m // tm),
        in_specs=[pl.BlockSpec((tm, k), lambda j, i: (i, 0)),
                  pl.BlockSpec((k, tn), lambda j, i: (0, col0 // tn + j)),
                  pl.BlockSpec((k, tn), lambda j, i: (0, (col0 + n) // tn + j))],
        out_specs=pl.BlockSpec((tm, tn), lambda j, i: (i, j)),
        compiler_params=_cparams(2),
        name="proj_glu",
    )(h, w_packed, w_packed)


def _gate_transform(x, exp_a, dt_bias, f_bias, idx):
    beta = _sigmoid(x)
    log_a = -exp_a * _softplus(x + dt_bias)
    log_f = -_softplus(-(x + f_bias))
    return jnp.where(idx < 8, beta, jnp.where(idx < 16, log_a, log_f))


def _small_kernel(h_ref, w_ref, pc_ref, o_ref, oc_ref):
    col = _dot(h_ref[...], w_ref[...])
    pc = pc_ref[...]
    cidx = lax.broadcasted_iota(jnp.int32, col.shape, 1)
    col = _gate_transform(col, pc[0:1, :], pc[1:2, :], pc[2:3, :], cidx)
    o_ref[...] = col
    row = col.T
    for c in range(row.shape[1] // CHUNK):
        oc_ref[c] = row[:N_SMALL_T, c * CHUNK:(c + 1) * CHUNK]


def _small_proj(h, w_packed, col0, prm_c, tm):
    m, k = h.shape
    return pl.pallas_call(
        _small_kernel,
        out_shape=(jax.ShapeDtypeStruct((m, N_SMALL), F32),
                   jax.ShapeDtypeStruct((m // CHUNK, N_SMALL_T, CHUNK), F32)),
        grid=(m // tm,),
        in_specs=[pl.BlockSpec((tm, k), lambda i: (i, 0)),
                  pl.BlockSpec((k, N_SMALL), lambda i: (0, col0 // N_SMALL)),
                  pl.BlockSpec((8, N_SMALL), lambda i: (0, 0))],
        out_specs=(pl.BlockSpec((tm, N_SMALL), lambda i: (i, 0)),
                   pl.BlockSpec((tm // CHUNK, N_SMALL_T, CHUNK), lambda i: (i, 0, 0))),
        compiler_params=_cparams(1),
        name="small_proj",
    )(h, w_packed, prm_c)


def _split3(x):
    a = x.astype(BF16)
    r1 = x - a.astype(F32)
    b = r1.astype(BF16)
    c = (r1 - b.astype(F32)).astype(BF16)
    return a, b, c


LOG2_E = 1.4426950408889634
N_SPLIT = 3


def _cumsum_kernel(x_ref, o_ref, carry_ref):
    ts = x_ref.shape[1]
    nh = o_ref.shape[1]
    blk = 128

    @pl.when(pl.program_id(1) == 0)
    def _():
        carry_ref[...] = jnp.zeros_like(carry_ref)

    r = lax.broadcasted_iota(jnp.int32, (blk, blk), 0)
    c = lax.broadcasted_iota(jnp.int32, (blk, blk), 1)
    tril = jnp.where(r >= c, 1.0, 0.0).astype(BF16)
    er = lax.broadcasted_iota(jnp.int32, (blk, nh * blk), 0)
    ec = lax.broadcasted_iota(jnp.int32, (blk, nh * blk), 1)
    place = [jnp.where((er == 2 * nh + ec // blk) & (ec % blk == t), 1.0, 0.0).astype(BF16)
             for t in range(N_SPLIT)]

    carry = carry_ref[0:1, :]
    for j in range(ts // blk):
        rows = slice(j * blk, (j + 1) * blk)
        x1, x2, x3 = _split3(x_ref[0, rows, :])
        run = _dot(tril, x1) + _dot(tril, x2) + _dot(tril, x3) + carry
        carry = run[blk - 1:blk, :]
        parts = _split3(run * LOG2_E)
        rep = _dot(parts[0], place[0]) + _dot(parts[1], place[1]) + _dot(parts[2], place[2])
        for h in range(nh):
            o_ref[0, h, rows, :] = rep[:, h * blk:(h + 1) * blk].astype(BF16)
    carry_ref[0:1, :] = carry


def _forget_cumsum(small_c3, nh, ts):
    bsz, seq, _ = small_c3.shape
    return pl.pallas_call(
        _cumsum_kernel,
        out_shape=jax.ShapeDtypeStruct((bsz, nh, seq, 128), BF16),
        grid=(bsz, seq // ts),
        in_specs=[pl.BlockSpec((1, ts, N_SMALL), lambda b, i: (b, i, 0))],
        out_specs=pl.BlockSpec((1, nh, ts, 128), lambda b, i: (b, 0, i, 0)),
        scratch_shapes=[pltpu.VMEM((8, N_SMALL), F32)],
        compiler_params=_cparams(2),
        name="forget_cumsum",
    )(small_c3)


GDN_CHUNKS_PER_ITER = 2


def _gdn_local_kernel(qkv_ref, halo_ref, cw_ref, sc_ref, sr_ref,
                      u0_ref, w_ref, qd_ref, kt_ref, qk_ref, dec_ref, xx_ref, act_ref):
    i = pl.program_id(1)
    ts = qkv_ref.shape[1]
    bw = u0_ref.shape[2]
    nh = bw // HEAD_DIM
    halo = halo_ref[0]
    xx_ref[0:8, :] = jnp.where(i > 0, halo, 0.0)
    xx_ref[8:, :] = qkv_ref[0]

    cw = cw_ref[...]
    for rb in range(ts // CHUNK):
        acc = None
        for k in range(SHORT_CONV):
            start = rb * CHUNK + (8 - (SHORT_CONV - 1)) + k
            term = cw[k:k + 1, :] * xx_ref[start:start + CHUNK, :]
            acc = term if acc is None else acc + term
        act_ref[rb * CHUNK:(rb + 1) * CHUNK, :] = _silu(acc)

    r = lax.broadcasted_iota(jnp.int32, (CHUNK, CHUNK), 0)
    c = lax.broadcasted_iota(jnp.int32, (CHUNK, CHUNK), 1)
    incl = r >= c
    strict = r > c
    tril = incl.astype(F32)
    triu = (r <= c).astype(F32)
    eye = jnp.where(r == c, 1.0, 0.0).astype(F32)
    nchunks = ts // CHUNK
    cpi = GDN_CHUNKS_PER_ITER if nchunks % GDN_CHUNKS_PER_ITER == 0 else 1

    def iter_body(it, _):
        units = []
        for cc in range(cpi):
            ci = it * cpi + cc
            base = pl.multiple_of(ci * CHUNK, CHUNK)
            rows = pl.ds(base, CHUNK)
            sc = sc_ref[0, rows, :]
            g_col = _dot_f32(tril, sc)
            g_row = _dot_f32(sr_ref[ci], triu)
            for h in range(nh):
                q = act_ref[rows, h * HEAD_DIM:(h + 1) * HEAD_DIM]
                k_ = act_ref[rows, bw + h * HEAD_DIM: bw + (h + 1) * HEAD_DIM]
                v = act_ref[rows, 2 * bw + h * HEAD_DIM: 2 * bw + (h + 1) * HEAD_DIM]
                q = q * (lax.rsqrt(jnp.sum(q * q, axis=-1, keepdims=True) + NORM_EPS) * HEAD_DIM ** -0.5)
                k_ = k_ * lax.rsqrt(jnp.sum(k_ * k_, axis=-1, keepdims=True) + NORM_EPS)
                beta = sc[:, h:h + 1]
                g = g_col[:, nh + h:nh + h + 1]
                g_last = g_col[CHUNK - 1:CHUNK, nh + h:nh + h + 1]
                gr = g_row[nh + h:nh + h + 1, :]
                units.append(dict(
                    ci=ci, rows=rows, h=h, q=q, k=k_, v=v, beta=beta, g=g, g_last=g_last,
                    decay=jnp.exp(jnp.where(incl, g - gr, -jnp.inf)),
                    eg=jnp.exp(g), kb=k_.astype(BF16), qb=q.astype(BF16)))

        for u in units:
            u["kq"] = _dot_nt(jnp.concatenate([u["kb"], u["qb"]], axis=0), u["kb"])
        for u in units:
            u["qk"] = u["kq"][CHUNK:] * u["decay"]
            a = jnp.where(strict, -(u["beta"] * u["kq"][:CHUNK] * u["decay"]), 0.0)
            u["x"] = eye + a
            u["p"] = a
        for u in units:
            pb = u["p"].astype(BF16)
            u["p"] = _dot(pb, pb)
        m = 2
        while 2 * m < CHUNK:
            for u in units:
                pb = u["p"].astype(BF16)
                out = _dot(jnp.concatenate([u["x"].astype(BF16), pb], axis=0), pb)
                u["x"] = u["x"] + out[:CHUNK]
                u["p"] = out[CHUNK:]
            m *= 2
        for u in units:
            u["x"] = u["x"] + _dot(u["x"].astype(BF16), u["p"].astype(BF16))
        for u in units:
            rhs = jnp.concatenate([u["v"] * u["beta"], u["k"] * (u["beta"] * u["eg"])], axis=1)
            u["uw"] = _dot(u["x"].astype(BF16), rhs.astype(BF16))

        for u in units:
            h, rows = u["h"], u["rows"]
            sl = slice(h * HEAD_DIM, (h + 1) * HEAD_DIM)
            u0_ref[0, rows, sl] = u["uw"][:, :HEAD_DIM]
            w_ref[0, rows, sl] = u["uw"][:, HEAD_DIM:].astype(BF16)
            qd_ref[0, rows, sl] = (u["q"] * u["eg"]).astype(BF16)
            kt_ref[0, rows, sl] = (u["k"] * jnp.exp(u["g_last"] - u["g"])).astype(BF16)
            qk_ref[0, rows, h * CHUNK:(h + 1) * CHUNK] = u["qk"].astype(BF16)
            dec_ref[0, u["ci"], :, sl] = jnp.broadcast_to(jnp.exp(u["g_last"]), (1, HEAD_DIM))
        return 0

    lax.fori_loop(0, nchunks // cpi, iter_body, 0)


def _gdn_local(qkv_a, conv_w, small_c, small_ch, ts):
    bsz, seq, n3 = qkv_a.shape
    bw = n3 // 3
    nh = bw // HEAD_DIM
    nt = seq // ts
    nc = ts // CHUNK
    big = lambda b, i: (b, i, 0)
    return pl.pallas_call(
        _gdn_local_kernel,
        out_shape=(jax.ShapeDtypeStruct((bsz, seq, bw), F32),
                   jax.ShapeDtypeStruct((bsz, seq, bw), BF16),
                   jax.ShapeDtypeStruct((bsz, seq, bw), BF16),
                   jax.ShapeDtypeStruct((bsz, seq, bw), BF16),
                   jax.ShapeDtypeStruct((bsz, seq, nh * CHUNK), BF16),
                   jax.ShapeDtypeStruct((bsz, seq // CHUNK, 1, bw), F32)),
        grid=(bsz, nt),
        in_specs=[pl.BlockSpec((1, ts, n3), big),
                  pl.BlockSpec((1, 8, n3), lambda b, i: (b, jnp.maximum(i * (ts // 8) - 1, 0), 0)),
                  pl.BlockSpec((SHORT_CONV, n3), lambda b, i: (0, 0)),
                  pl.BlockSpec((1, ts, N_SMALL), big),
                  pl.BlockSpec((nc, N_SMALL_T, CHUNK), lambda b, i: (b * nt + i, 0, 0))],
        out_specs=(pl.BlockSpec((1, ts, bw), big),
                   pl.BlockSpec((1, ts, bw), big),
                   pl.BlockSpec((1, ts, bw), big),
                   pl.BlockSpec((1, ts, bw), big),
                   pl.BlockSpec((1, ts, nh * CHUNK), big),
                   pl.BlockSpec((1, nc, 1, bw), lambda b, i: (b, i, 0, 0))),
        scratch_shapes=[pltpu.VMEM((ts + 8, n3), F32), pltpu.VMEM((ts, n3), F32)],
        compiler_params=_cparams(2),
        name="gdn_local",
    )(qkv_a, qkv_a, conv_w, small_c.reshape(bsz, seq, N_SMALL), small_ch)


def _gdn_scan_kernel(u0_ref, w_ref, qd_ref, kt_ref, qk_ref, dec_ref, z_ref, nw_ref, y_ref, state_ref):
    i = pl.program_id(0)
    bsz, ts, bw = u0_ref.shape
    nh = bw // HEAD_DIM

    @pl.when(i == 0)
    def _():
        state_ref[...] = jnp.zeros_like(state_ref)

    nw = nw_ref[...]

    def chunk_body(ci, _):
        rows = pl.ds(pl.multiple_of(ci * CHUNK, CHUNK), CHUNK)
        units = []
        for b in range(bsz):
            for h in range(nh):
                sl = slice(h * HEAD_DIM, (h + 1) * HEAD_DIM)
                state = state_ref[b * nh + h]
                lhs = jnp.concatenate([w_ref[b, rows, sl], qd_ref[b, rows, sl]], axis=0)
                units.append(dict(b=b, h=h, sl=sl, state=state, ws=_dot(lhs, state.astype(BF16))))
        for u in units:
            b, h, sl = u["b"], u["h"], u["sl"]
            ub = (u0_ref[b, rows, sl] - u["ws"][:CHUNK]).astype(BF16)
            u["o"] = u["ws"][CHUNK:] + _dot(qk_ref[b, rows, h * CHUNK:(h + 1) * CHUNK], ub)
            u["ds"] = _dot_tn(kt_ref[b, rows, sl], ub)
        for u in units:
            b, h, sl = u["b"], u["h"], u["sl"]
            state_ref[b * nh + h] = u["state"] * dec_ref[b, ci, :, sl] + u["ds"]
            o = u["o"]
            o = o * lax.rsqrt(jnp.mean(o * o, axis=-1, keepdims=True) + NORM_EPS)
            y_ref[b, rows, sl] = (o * nw * z_ref[b, rows, sl].astype(F32)).astype(BF16)
        return 0

    lax.fori_loop(0, ts // CHUNK, chunk_body, 0)


def _gdn_scan(u0, w, qd, kt, qk, dec, z_all, o_norm_w, ts):
    bsz, seq, bw = u0.shape
    nh = bw // HEAD_DIM
    big = lambda i: (0, i, 0)
    return pl.pallas_call(
        _gdn_scan_kernel,
        out_shape=jax.ShapeDtypeStruct((bsz, seq, bw), BF16),
        grid=(seq // ts,),
        in_specs=[pl.BlockSpec((bsz, ts, bw), big),
                  pl.BlockSpec((bsz, ts, bw), big),
                  pl.BlockSpec((bsz, ts, bw), big),
                  pl.BlockSpec((bsz, ts, bw), big),
                  pl.BlockSpec((bsz, ts, nh * CHUNK), big),
                  pl.BlockSpec((bsz, ts // CHUNK, 1, bw), lambda i: (0, i, 0, 0)),
                  pl.BlockSpec((bsz, ts, bw), big),
                  pl.BlockSpec((1, HEAD_DIM), lambda i: (0, 0))],
        out_specs=pl.BlockSpec((bsz, ts, bw), big),
        scratch_shapes=[pltpu.VMEM((bsz * nh, HEAD_DIM, HEAD_DIM), F32)],
        compiler_params=_cparams(1),
        name="gdn_scan",
    )(u0, w, qd, kt, qk, dec, z_all, o_norm_w.reshape(1, HEAD_DIM))


def _conformer_kernel(u_ref, halo_ref, z_ref, cw_ref, cb_ref, lw_ref, lb_ref, y_ref, xx_ref):
    i = pl.program_id(1)
    ts = u_ref.shape[1]
    pad = halo_ref.shape[1]
    xx_ref[0:pad, :] = jnp.where(i > 0, halo_ref[0], 0.0)
    xx_ref[pad:, :] = u_ref[0]
    cw = cw_ref[...]
    off = pad - (CONF_CONV - 1)
    acc = None
    for r in range(8):
        rows = ts if r == 0 else ts + 8
        y = None
        for m in range((pad + 8) // 8):
            k = 8 * m + r - off
            if 0 <= k < CONF_CONV:
                term = cw[k:k + 1, :] * xx_ref[8 * m:8 * m + rows, :]
                y = term if y is None else y + term
        shifted = y[r:r + ts, :]
        acc = shifted if acc is None else acc + shifted
    u = acc + cb_ref[...]
    xc = u - jnp.mean(u, axis=-1, keepdims=True)
    y = xc * lax.rsqrt(jnp.mean(xc * xc, axis=-1, keepdims=True) + NORM_EPS)
    y = _silu(y * lw_ref[...] + lb_ref[...])
    y_ref[0] = (y * z_ref[0].astype(F32)).astype(BF16)


def _conformer(u, z_all, conv_w, conv_b, ln_w, ln_b, ts):
    bsz, seq, bw = u.shape
    pad = 32
    big = lambda b, i: (b, i, 0)
    vec = lambda b, i: (0, 0)
    return pl.pallas_call(
        _conformer_kernel,
        out_shape=jax.ShapeDtypeStruct((bsz, seq, bw), BF16),
        grid=(bsz, seq // ts),
        in_specs=[pl.BlockSpec((1, ts, bw), big),
                  pl.BlockSpec((1, pad, bw), lambda b, i: (b, jnp.maximum(i * (ts // pad) - 1, 0), 0)),
                  pl.BlockSpec((1, ts, bw), lambda b, i: (b, i, 1)),
                  pl.BlockSpec((CONF_CONV, bw), vec),
                  pl.BlockSpec((1, bw), vec),
                  pl.BlockSpec((1, bw), vec),
                  pl.BlockSpec((1, bw), vec)],
        out_specs=pl.BlockSpec((1, ts, bw), big),
        scratch_shapes=[pltpu.VMEM((ts + pad, bw), F32)],
        compiler_params=_cparams(2),
        name="conformer",
    )(u, u, z_all, conv_w, conv_b.reshape(1, bw), ln_w.reshape(1, bw), ln_b.reshape(1, bw))


FOX_SUB = 256


def _fox_kernel(qi_ref, kj_ref, qt_ref, k_ref, vt_ref, c_ref, z_ref, y_ref, m_ref, l_ref, acc_ref):
    t = pl.program_id(2)
    i = qi_ref[t]
    j = kj_ref[t]
    tq = qt_ref.shape[2]
    tk = k_ref.shape[1]
    nsub = tq // FOX_SUB

    @pl.when(j == 0)
    def _():
        m_ref[...] = jnp.full_like(m_ref, -jnp.inf)
        l_ref[...] = jnp.zeros_like(l_ref)
        acc_ref[...] = jnp.zeros_like(acc_ref)

    gate_rows = lax.broadcasted_iota(jnp.int32, (c_ref.shape[3], FOX_SUB), 0)
    minus_gate = jnp.where(gate_rows < N_SPLIT, -1.0, 0.0).astype(BF16)

    def step(diag):
        rows = [min((n + 1) * FOX_SUB, tk) if diag else tk for n in range(nsub)]
        s_all = []
        for n in range(nsub):
            k_aug = jnp.concatenate([k_ref[0, :rows[n], :], c_ref[0, 0, :rows[n], :]], axis=1)
            q_aug = jnp.concatenate([qt_ref[0, :, n * FOX_SUB:(n + 1) * FOX_SUB], minus_gate], axis=0)
            s_all.append(_dot(k_aug, q_aug))
        for n in range(nsub):
            cols = slice(n * FOX_SUB, (n + 1) * FOX_SUB)
            s = s_all[n]
            if diag:
                kpos = lax.broadcasted_iota(jnp.int32, s.shape, 0)
                qpos = lax.broadcasted_iota(jnp.int32, s.shape, 1) + n * FOX_SUB
                s = jnp.where(kpos <= qpos, s, -jnp.inf)
            m_prev = m_ref[:, cols]
            m_new = jnp.maximum(m_prev, jnp.max(s, axis=0, keepdims=True))
            p = jnp.exp2(s - m_new)
            alpha = jnp.exp2(m_prev - m_new)
            l_ref[:, cols] = alpha * l_ref[:, cols] + jnp.sum(p, axis=0, keepdims=True)
            acc_ref[:, cols] = alpha * acc_ref[:, cols] + _dot(vt_ref[0, :, :rows[n]], p.astype(BF16))
            m_ref[:, cols] = m_new

    @pl.when(j < i)
    def _():
        step(False)

    @pl.when(j == i)
    def _():
        step(True)
        o = (acc_ref[...] / l_ref[...]).T
        y_ref[0] = (o * z_ref[0].astype(F32)).astype(BF16)


def _fox_attention(q_t, k, v_t, c_split, z_all, t):
    bsz, bw, seq = q_t.shape
    nh = bw // HEAD_DIM
    nt = seq // t
    steps = [(i, j) for i in range(nt) for j in range(i + 1)]
    qi = jnp.asarray([s[0] for s in steps], jnp.int32)
    kj = jnp.asarray([s[1] for s in steps], jnp.int32)
    grid_spec = pltpu.PrefetchScalarGridSpec(
        num_scalar_prefetch=2,
        grid=(bsz, nh, len(steps)),
        in_specs=[pl.BlockSpec((1, HEAD_DIM, t), lambda b, h, s, qi, kj: (b, h, qi[s])),
                  pl.BlockSpec((1, t, HEAD_DIM), lambda b, h, s, qi, kj: (b, kj[s], h)),
                  pl.BlockSpec((1, HEAD_DIM, t), lambda b, h, s, qi, kj: (b, h, kj[s])),
                  pl.BlockSpec((1, 1, t, 128), lambda b, h, s, qi, kj: (b, h, kj[s], 0)),
                  pl.BlockSpec((1, t, HEAD_DIM), lambda b, h, s, qi, kj: (b, qi[s], 2 * nh + h))],
        out_specs=pl.BlockSpec((1, t, HEAD_DIM), lambda b, h, s, qi, kj: (b, qi[s], h)),
        scratch_shapes=[pltpu.VMEM((1, t), F32), pltpu.VMEM((1, t), F32), pltpu.VMEM((HEAD_DIM, t), F32)],
    )
    return pl.pallas_call(
        _fox_kernel,
        out_shape=jax.ShapeDtypeStruct((bsz, seq, bw), BF16),
        grid_spec=grid_spec,
        compiler_params=_cparams(3),
        name="fox_attention",
    )(qi, kj, q_t, k, v_t, c_split, z_all)


def _merge_kernel(h_ref, ya_ref, yb_ref, yc_ref, wg0_ref, wg1_ref, wg2_ref, wb0_ref, wb1_ref, wb2_ref, o_ref):
    h = h_ref[...]
    acc = None
    for y_ref, wg_ref, wb_ref in ((ya_ref, wg0_ref, wb0_ref), (yb_ref, wg1_ref, wb1_ref), (yc_ref, wg2_ref, wb2_ref)):
        term = _sigmoid(_dot(h, wg_ref[...])) * _dot(y_ref[...], wb_ref[0])
        acc = term if acc is None else acc + term
    o_ref[...] = acc.astype(BF16)


def _merge(h, ya, yb, yc, w_packed, gate_col0, w_branch, tm, tn):
    m, d = h.shape
    bw = ya.shape[1]
    nj = d // tn
    g0 = gate_col0 // tn
    row = lambda j, i: (i, 0)
    return pl.pallas_call(
        _merge_kernel,
        out_shape=jax.ShapeDtypeStruct((m, d), BF16),
        grid=(nj, m // tm),
        in_specs=[pl.BlockSpec((tm, d), row),
                  pl.BlockSpec((tm, bw), row), pl.BlockSpec((tm, bw), row), pl.BlockSpec((tm, bw), row),
                  pl.BlockSpec((d, tn), lambda j, i: (0, g0 + j)),
                  pl.BlockSpec((d, tn), lambda j, i: (0, g0 + nj + j)),
                  pl.BlockSpec((d, tn), lambda j, i: (0, g0 + 2 * nj + j)),
                  pl.BlockSpec((1, bw, tn), lambda j, i: (0, 0, j)),
                  pl.BlockSpec((1, bw, tn), lambda j, i: (1, 0, j)),
                  pl.BlockSpec((1, bw, tn), lambda j, i: (2, 0, j))],
        out_specs=pl.BlockSpec((tm, tn), lambda j, i: (i, j)),
        compiler_params=_cparams(2),
        name="gated_merge",
    )(h, ya, yb, yc, w_packed, w_packed, w_packed, w_branch, w_branch, w_branch)


def _out_kernel(m_ref, w_ref, x_ref, pw_ref, o_ref):
    out = _dot(m_ref[...], w_ref[...])
    y = out * lax.rsqrt(jnp.mean(out * out, axis=-1, keepdims=True) + NORM_EPS)
    o_ref[...] = x_ref[...] + y * pw_ref[...]


def _out_proj(merged, w_out, x2, post_w, tm):
    m, d = x2.shape
    return pl.pallas_call(
        _out_kernel,
        out_shape=jax.ShapeDtypeStruct((m, d), F32),
        grid=(m // tm,),
        in_specs=[pl.BlockSpec((tm, d), lambda i: (i, 0)),
                  pl.BlockSpec((d, d), lambda i: (0, 0)),
                  pl.BlockSpec((tm, d), lambda i: (i, 0)),
                  pl.BlockSpec((1, d), lambda i: (0, 0))],
        out_specs=pl.BlockSpec((tm, d), lambda i: (i, 0)),
        compiler_params=_cparams(1),
        name="out_proj",
    )(merged, w_out, x2, post_w.reshape(1, d))


def _layer(x2, bsz, seq, p):
    m, d = x2.shape
    bw = d // 2
    tm = min(1024, m)
    h = _prenorm(x2, p["pre_w"], min(512, m))

    wp = p["w_packed"]
    col = _packed_columns(d)
    qkv_a = _proj(h, wp, col["qkv_a"], 3 * bw, epilogue="f32", tm=tm, tn=1024, out_dtype=F32)
    z_all = _proj(h, wp, col["z"], 3 * bw, epilogue="silu_bf16", tm=tm, tn=1024, out_dtype=BF16)
    u_b = _proj_glu(h, wp, col["glu"], bw, tm=tm, tn=512)
    q_t = _proj_t(h, p["w_q_t"], bsz=bsz, tm=min(1024, seq), tn=1024, scale=HEAD_DIM ** -0.5 * LOG2_E)
    k_c = _proj(h, wp, col["k_c"], bw, epilogue="bf16", tm=tm, tn=1024, out_dtype=BF16)
    v_t = _proj_t(h, p["w_v_t"], bsz=bsz, tm=min(1024, seq), tn=1024)
    small_c, small_ch = _small_proj(h, wp, col["small"], p["prm_c"], min(512, seq))

    z3 = z_all.reshape(bsz, seq, 3 * bw)
    u0, w, qd, kt, qk, dec = _gdn_local(qkv_a.reshape(bsz, seq, 3 * bw), p["conv_qkv_w"], small_c, small_ch,
                                        min(256, seq))
    y_a = _gdn_scan(u0, w, qd, kt, qk, dec, z3, p["o_norm_w"], min(512, seq))

    y_b = _conformer(u_b.reshape(bsz, seq, bw), z3, p["conv_w"], p["conv_b"], p["ln_w"], p["ln_b"], min(256, seq))

    c_split = _forget_cumsum(small_c.reshape(bsz, seq, N_SMALL), bw // HEAD_DIM, min(1024, seq))
    y_c = _fox_attention(q_t, k_c.reshape(bsz, seq, bw), v_t, c_split, z3, min(1024, seq))

    merged = _merge(h, y_a.reshape(m, bw), y_b.reshape(m, bw), y_c.reshape(m, bw),
                    wp, col["gate"], p["w_branch"], min(512, m), 512)
    return _out_proj(merged, p["w_out"], x2, p["post_w"], min(256, m))


def _packed_columns(d):
    bw = d // 2
    names = (("qkv_a", 3 * bw), ("z", 3 * bw), ("glu", 2 * bw), ("q_c", bw), ("k_c", bw), ("v_c", bw),
             ("gate", N_BRANCH * d), ("small", N_SMALL))
    col, off = {}, 0
    for name, width in names:
        col[name] = off
        off += width
    col["total"] = off
    return col


def _prep_params(pre_norm_w, post_norm_w, w_in, conv_qkv_w, a_log, dt_bias, o_norm_w,
                 conv_w, conv_b, ln_w, ln_b, f_bias, w_branch, w_out):
    depth, d, _ = w_in.shape
    bw = d // 2
    nh = bw // HEAD_DIM
    sizes = (3 * bw, bw, nh, nh, 2 * bw, bw, 3 * bw, bw, nh, N_BRANCH * d)
    offs = [0]
    for s in sizes:
        offs.append(offs[-1] + s)
    seg = lambda k: w_in[:, :, offs[k]:offs[k + 1]].astype(BF16)
    w_packed = jnp.concatenate(
        [seg(0), seg(1), seg(5), seg(7), seg(4), seg(6), seg(9), seg(2), seg(3), seg(8),
         jnp.zeros((depth, d, N_SMALL - 3 * nh), BF16)], axis=2)
    col = _packed_columns(d)
    assert w_packed.shape[2] == col["total"]
    zeros = jnp.zeros((depth, nh), F32)
    ones = jnp.ones((depth, nh), F32)
    vec24 = lambda a, b, c: jnp.concatenate([a, b, c], axis=1)
    prm = jnp.stack([vec24(ones, jnp.exp(a_log.astype(F32)), ones),
                     vec24(zeros, dt_bias.astype(F32), zeros),
                     vec24(zeros, zeros, f_bias.astype(F32))], axis=1)
    prm_c = jnp.pad(prm, ((0, 0), (0, 8 - 3), (0, N_SMALL - 3 * nh)))
    return {
        "pre_w": pre_norm_w, "post_w": post_norm_w,
        "w_packed": w_packed, "prm_c": prm_c,
        "w_q_t": jnp.swapaxes(w_packed[:, :, col["q_c"]:col["q_c"] + bw], 1, 2),
        "w_v_t": jnp.swapaxes(w_packed[:, :, col["v_c"]:col["v_c"] + bw], 1, 2),
        "conv_qkv_w": conv_qkv_w, "o_norm_w": o_norm_w,
        "conv_w": conv_w, "conv_b": conv_b, "ln_w": ln_w, "ln_b": ln_b,
        "w_branch": w_branch.astype(BF16), "w_out": w_out.astype(BF16),
    }


def kernel(x, pre_norm_w, post_norm_w, w_in, conv_qkv_w, a_log, dt_bias, o_norm_w, conv_w, conv_b, ln_w, ln_b, f_bias, w_branch, w_out):
    bsz, seq, d = x.shape
    depth = w_in.shape[0]
    params = _prep_params(pre_norm_w, post_norm_w, w_in, conv_qkv_w, a_log, dt_bias, o_norm_w,
                          conv_w, conv_b, ln_w, ln_b, f_bias, w_branch, w_out)
    x2 = x.reshape(bsz * seq, d)
    for l in range(depth):
        x2 = _layer(x2, bsz, seq, {k: v[l] for k, v in params.items()})
    return x2.reshape(bsz, seq, d)
```

```python
import functools

import jax
import jax.numpy as jnp
from jax import lax
from jax.experimental import pallas as pl
from jax.experimental.pallas import tpu as pltpu

F32 = jnp.float32
BF16 = jnp.bfloat16

HEAD_DIM = 128
CHUNK = 64
SHORT_CONV = 4
CONF_CONV = 31
NORM_EPS = 1e-6
N_BRANCH = 3
N_SMALL = 128
N_SMALL_T = 32
V7X_VMEM_LIMIT = 48 * 1024 * 1024


def _cparams(n_axes):
    return pltpu.CompilerParams(dimension_semantics=("arbitrary",) * n_axes,
                                vmem_limit_bytes=V7X_VMEM_LIMIT)


def _sigmoid(x):
    return 1.0 / (1.0 + jnp.exp(-x))


def _silu(x):
    return x * _sigmoid(x)


def _softplus(x):
    return jnp.maximum(x, 0.0) + jnp.log(1.0 + jnp.exp(-jnp.abs(x)))


def _dot(a, b):
    return jnp.dot(a, b, preferred_element_type=F32)


def _dot_nt(a, b):
    return lax.dot_general(a, b, (((1,), (1,)), ((), ())), preferred_element_type=F32)


def _dot_tn(a, b):
    return lax.dot_general(a, b, (((0,), (0,)), ((), ())), preferred_element_type=F32)


def _dot_f32(a, b):
    return jnp.dot(a, b, preferred_element_type=F32, precision=lax.Precision.HIGHEST)


def _prenorm_kernel(x_ref, w_ref, h_ref):
    x = x_ref[...]
    y = x * lax.rsqrt(jnp.mean(x * x, axis=-1, keepdims=True) + NORM_EPS)
    h_ref[...] = (y * w_ref[...]).astype(BF16)


def _prenorm(x2, w, tm):
    m, d = x2.shape
    return pl.pallas_call(
        _prenorm_kernel,
        out_shape=jax.ShapeDtypeStruct((m, d), BF16),
        grid=(m // tm,),
        in_specs=[pl.BlockSpec((tm, d), lambda i: (i, 0)),
                  pl.BlockSpec((1, d), lambda i: (0, 0))],
        out_specs=pl.BlockSpec((tm, d), lambda i: (i, 0)),
        compiler_params=_cparams(1),
        name="prenorm",
    )(x2, w.reshape(1, d))


def _proj_kernel(h_ref, w_ref, o_ref, *, epilogue):
    acc = _dot(h_ref[...], w_ref[...])
    if epilogue == "f32":
        o_ref[...] = acc
    elif epilogue == "silu_bf16":
        o_ref[...] = _silu(acc).astype(BF16)
    elif epilogue == "bf16":
        o_ref[...] = acc.astype(BF16)
    else:
        raise ValueError(epilogue)


def _proj(h, w_packed, seg_cols, seg_width, *, epilogue, tm, tn, out_dtype):
    m, k = h.shape
    n = len(seg_cols) * seg_width
    per_seg = seg_width // tn

    def w_tile(j):
        tile = seg_cols[0] // tn + j
        for s in range(1, len(seg_cols)):
            tile = jnp.where(j >= s * per_seg, seg_cols[s] // tn + j - s * per_seg, tile)
        return tile
    return pl.pallas_call(
        functools.partial(_proj_kernel, epilogue=epilogue),
        out_shape=jax.ShapeDtypeStruct((m, n), out_dtype),
        grid=(n // tn, m // tm),
        in_specs=[pl.BlockSpec((tm, k), lambda j, i: (i, 0)),
                  pl.BlockSpec((k, tn), lambda j, i: (0, w_tile(j)))],
        out_specs=pl.BlockSpec((tm, tn), lambda j, i: (i, j)),
        compiler_params=_cparams(2),
        name="proj_" + epilogue,
    )(h, w_packed)


def _proj_t_kernel(wt_ref, h_ref, o_ref, *, scale):
    o_ref[0] = (_dot_nt(wt_ref[...], h_ref[...]) * scale).astype(BF16)


def _proj_t(h, w_t, *, bsz, tm, tn, scale=1.0):
    m, k = h.shape
    n = w_t.shape[0]
    seq = m // bsz
    nts = seq // tm
    return pl.pallas_call(
        functools.partial(_proj_t_kernel, scale=scale),
        out_shape=jax.ShapeDtypeStruct((bsz, n, seq), BF16),
        grid=(n // tn, m // tm),
        in_specs=[pl.BlockSpec((tn, k), lambda j, i: (j, 0)),
                  pl.BlockSpec((tm, k), lambda j, i: (i, 0))],
        out_specs=pl.BlockSpec((1, tn, tm), lambda j, i: (i // nts, j, i % nts)),
        compiler_params=_cparams(2),
        name="proj_t",
    )(w_t, h)


def _glu_kernel(h_ref, wv_ref, wg_ref, o_ref):
    h = h_ref[...]
    o_ref[...] = _dot(h, wv_ref[...]) * _sigmoid(_dot(h, wg_ref[...]))


def _proj_glu(h, w_packed, col0, n, *, tm, tn):
    m, k = h.shape
    return pl.pallas_call(
        _glu_kernel,
        out_shape=jax.ShapeDtypeStruct((m, n), F32),
        grid=(n // tn, m // tm),
        in_specs=[pl.BlockSpec((tm, k), lambda j, i: (i, 0)),
                  pl.BlockSpec((k, tn), lambda j, i: (0, col0 // tn + j)),
                  pl.BlockSpec((k, tn), lambda j, i: (0, (col0 + n) // tn + j))],
        out_specs=pl.BlockSpec((tm, tn), lambda j, i: (i, j)),
        compiler_params=_cparams(2),
        name="proj_glu",
    )(h, w_packed, w_packed)


def _gate_transform(x, exp_a, dt_bias, f_bias, idx):
    beta = _sigmoid(x)
    log_a = -exp_a * _softplus(x + dt_bias)
    log_f = -_softplus(-(x + f_bias))
    return jnp.where(idx < 8, beta, jnp.where(idx < 16, log_a, log_f))


def _small_kernel(h_ref, w_ref, pc_ref, o_ref, oc_ref):
    col = _dot(h_ref[...], w_ref[...])
    pc = pc_ref[...]
    cidx = lax.broadcasted_iota(jnp.int32, col.shape, 1)
    col = _gate_transform(col, pc[0:1, :], pc[1:2, :], pc[2:3, :], cidx)
    o_ref[...] = col
    row = col.T
    for c in range(row.shape[1] // CHUNK):
        oc_ref[c] = row[:N_SMALL_T, c * CHUNK:(c + 1) * CHUNK]


def _small_proj(h, w_small, prm_c, tm):
    m, k = h.shape
    return pl.pallas_call(
        _small_kernel,
        out_shape=(jax.ShapeDtypeStruct((m, N_SMALL), F32),
                   jax.ShapeDtypeStruct((m // CHUNK, N_SMALL_T, CHUNK), F32)),
        grid=(m // tm,),
        in_specs=[pl.BlockSpec((tm, k), lambda i: (i, 0)),
                  pl.BlockSpec((k, N_SMALL), lambda i: (0, 0)),
                  pl.BlockSpec((8, N_SMALL), lambda i: (0, 0))],
        out_specs=(pl.BlockSpec((tm, N_SMALL), lambda i: (i, 0)),
                   pl.BlockSpec((tm // CHUNK, N_SMALL_T, CHUNK), lambda i: (i, 0, 0))),
        compiler_params=_cparams(1),
        name="small_proj",
    )(h, w_small, prm_c)


def _split3(x):
    a = x.astype(BF16)
    r1 = x - a.astype(F32)
    b = r1.astype(BF16)
    c = (r1 - b.astype(F32)).astype(BF16)
    return a, b, c


LOG2_E = 1.4426950408889634
N_SPLIT = 3


def _cumsum_kernel(x_ref, o_ref, carry_ref):
    ts = x_ref.shape[1]
    nh = o_ref.shape[1]
    blk = 128

    @pl.when(pl.program_id(1) == 0)
    def _():
        carry_ref[...] = jnp.zeros_like(carry_ref)

    r = lax.broadcasted_iota(jnp.int32, (blk, blk), 0)
    c = lax.broadcasted_iota(jnp.int32, (blk, blk), 1)
    tril = jnp.where(r >= c, 1.0, 0.0).astype(BF16)
    er = lax.broadcasted_iota(jnp.int32, (blk, nh * blk), 0)
    ec = lax.broadcasted_iota(jnp.int32, (blk, nh * blk), 1)
    place = [jnp.where((er == 2 * nh + ec // blk) & (ec % blk == t), 1.0, 0.0).astype(BF16)
             for t in range(N_SPLIT)]

    carry = carry_ref[0:1, :]
    for j in range(ts // blk):
        rows = slice(j * blk, (j + 1) * blk)
        x1, x2, x3 = _split3(x_ref[0, rows, :])
        run = _dot(tril, x1) + _dot(tril, x2) + _dot(tril, x3) + carry
        carry = run[blk - 1:blk, :]
        parts = _split3(run * LOG2_E)
        rep = _dot(parts[0], place[0]) + _dot(parts[1], place[1]) + _dot(parts[2], place[2])
        for h in range(nh):
            o_ref[0, h, rows, :] = rep[:, h * blk:(h + 1) * blk].astype(BF16)
    carry_ref[0:1, :] = carry


def _forget_cumsum(small_c3, nh, ts):
    bsz, seq, _ = small_c3.shape
    return pl.pallas_call(
        _cumsum_kernel,
        out_shape=jax.ShapeDtypeStruct((bsz, nh, seq, 128), BF16),
        grid=(bsz, seq // ts),
        in_specs=[pl.BlockSpec((1, ts, N_SMALL), lambda b, i: (b, i, 0))],
        out_specs=pl.BlockSpec((1, nh, ts, 128), lambda b, i: (b, 0, i, 0)),
        scratch_shapes=[pltpu.VMEM((8, N_SMALL), F32)],
        compiler_params=_cparams(2),
        name="forget_cumsum",
    )(small_c3)


GDN_CHUNKS_PER_ITER = 2


def _gdn_local_kernel(qkv_ref, halo_ref, cw_ref, sc_ref, sr_ref,
                      u0_ref, w_ref, qd_ref, kt_ref, qk_ref, dec_ref, xx_ref, act_ref):
    i = pl.program_id(1)
    ts = qkv_ref.shape[1]
    bw = u0_ref.shape[2]
    nh = bw // HEAD_DIM
    halo = halo_ref[0]
    xx_ref[0:8, :] = jnp.where(i > 0, halo, 0.0)
    xx_ref[8:, :] = qkv_ref[0]

    cw = cw_ref[...]
    for rb in range(ts // CHUNK):
        acc = None
        for k in range(SHORT_CONV):
            start = rb * CHUNK + (8 - (SHORT_CONV - 1)) + k
            term = cw[k:k + 1, :] * xx_ref[start:start + CHUNK, :]
            acc = term if acc is None else acc + term
        act_ref[rb * CHUNK:(rb + 1) * CHUNK, :] = _silu(acc)

    r = lax.broadcasted_iota(jnp.int32, (CHUNK, CHUNK), 0)
    c = lax.broadcasted_iota(jnp.int32, (CHUNK, CHUNK), 1)
    incl = r >= c
    strict = r > c
    tril = incl.astype(F32)
    triu = (r <= c).astype(F32)
    eye = jnp.where(r == c, 1.0, 0.0).astype(F32)
    nchunks = ts // CHUNK
    cpi = GDN_CHUNKS_PER_ITER if nchunks % GDN_CHUNKS_PER_ITER == 0 else 1

    def iter_body(it, _):
        units = []
        for cc in range(cpi):
            ci = it * cpi + cc
            base = pl.multiple_of(ci * CHUNK, CHUNK)
            rows = pl.ds(base, CHUNK)
            sc = sc_ref[0, rows, :]
            g_col = _dot_f32(tril, sc)
            g_row = _dot_f32(sr_ref[ci], triu)
            for h in range(nh):
                q = act_ref[rows, h * HEAD_DIM:(h + 1) * HEAD_DIM]
                k_ = act_ref[rows, bw + h * HEAD_DIM: bw + (h + 1) * HEAD_DIM]
                v = act_ref[rows, 2 * bw + h * HEAD_DIM: 2 * bw + (h + 1) * HEAD_DIM]
                q = q * (lax.rsqrt(jnp.sum(q * q, axis=-1, keepdims=True) + NORM_EPS) * HEAD_DIM ** -0.5)
                k_ = k_ * lax.rsqrt(jnp.sum(k_ * k_, axis=-1, keepdims=True) + NORM_EPS)
                beta = sc[:, h:h + 1]
                g = g_col[:, nh + h:nh + h + 1]
                g_last = g_col[CHUNK - 1:CHUNK, nh + h:nh + h + 1]
                gr = g_row[nh + h:nh + h + 1, :]
                units.append(dict(
                    ci=ci, rows=rows, h=h, q=q, k=k_, v=v, beta=beta, g=g, g_last=g_last,
                    decay=jnp.exp(jnp.where(incl, g - gr, -jnp.inf)),
                    eg=jnp.exp(g), kb=k_.astype(BF16), qb=q.astype(BF16)))

        for u in units:
            u["kq"] = _dot_nt(jnp.concatenate([u["kb"], u["qb"]], axis=0), u["kb"])
        for u in units:
            u["qk"] = u["kq"][CHUNK:] * u["decay"]
            a = jnp.where(strict, -(u["beta"] * u["kq"][:CHUNK] * u["decay"]), 0.0)
            u["x"] = eye + a
            u["p"] = a
        for u in units:
            pb = u["p"].astype(BF16)
            u["p"] = _dot(pb, pb)
        m = 2
        while 2 * m < CHUNK:
            for u in units:
                pb = u["p"].astype(BF16)
                out = _dot(jnp.concatenate([u["x"].astype(BF16), pb], axis=0), pb)
                u["x"] = u["x"] + out[:CHUNK]
                u["p"] = out[CHUNK:]
            m *= 2
        for u in units:
            u["x"] = u["x"] + _dot(u["x"].astype(BF16), u["p"].astype(BF16))
        for u in units:
            rhs = jnp.concatenate([u["v"] * u["beta"], u["k"] * (u["beta"] * u["eg"])], axis=1)
            u["uw"] = _dot(u["x"].astype(BF16), rhs.astype(BF16))

        for u in units:
            h, rows = u["h"], u["rows"]
            sl = slice(h * HEAD_DIM, (h + 1) * HEAD_DIM)
            u0_ref[0, rows, sl] = u["uw"][:, :HEAD_DIM]
            w_ref[0, rows, sl] = u["uw"][:, HEAD_DIM:].astype(BF16)
            qd_ref[0, rows, sl] = (u["q"] * u["eg"]).astype(BF16)
            kt_ref[0, rows, sl] = (u["k"] * jnp.exp(u["g_last"] - u["g"])).astype(BF16)
            qk_ref[0, rows, h * CHUNK:(h + 1) * CHUNK] = u["qk"].astype(BF16)
            dec_ref[0, u["ci"], :, sl] = jnp.broadcast_to(jnp.exp(u["g_last"]), (1, HEAD_DIM))
        return 0

    lax.fori_loop(0, nchunks // cpi, iter_body, 0)


def _gdn_local(qkv_a, conv_w, small_c, small_ch, ts):
    bsz, seq, n3 = qkv_a.shape
    bw = n3 // 3
    nh = bw // HEAD_DIM
    nt = seq // ts
    nc = ts // CHUNK
    big = lambda b, i: (b, i, 0)
    return pl.pallas_call(
        _gdn_local_kernel,
        out_shape=(jax.ShapeDtypeStruct((bsz, seq, bw), F32),
                   jax.ShapeDtypeStruct((bsz, seq, bw), BF16),
                   jax.ShapeDtypeStruct((bsz, seq, bw), BF16),
                   jax.ShapeDtypeStruct((bsz, seq, bw), BF16),
                   jax.ShapeDtypeStruct((bsz, seq, nh * CHUNK), BF16),
                   jax.ShapeDtypeStruct((bsz, seq // CHUNK, 1, bw), F32)),
        grid=(bsz, nt),
        in_specs=[pl.BlockSpec((1, ts, n3), big),
                  pl.BlockSpec((1, 8, n3), lambda b, i: (b, jnp.maximum(i * (ts // 8) - 1, 0), 0)),
                  pl.BlockSpec((SHORT_CONV, n3), lambda b, i: (0, 0)),
                  pl.BlockSpec((1, ts, N_SMALL), big),
                  pl.BlockSpec((nc, N_SMALL_T, CHUNK), lambda b, i: (b * nt + i, 0, 0))],
        out_specs=(pl.BlockSpec((1, ts, bw), big),
                   pl.BlockSpec((1, ts, bw), big),
                   pl.BlockSpec((1, ts, bw), big),
                   pl.BlockSpec((1, ts, bw), big),
                   pl.BlockSpec((1, ts, nh * CHUNK), big),
                   pl.BlockSpec((1, nc, 1, bw), lambda b, i: (b, i, 0, 0))),
        scratch_shapes=[pltpu.VMEM((ts + 8, n3), F32), pltpu.VMEM((ts, n3), F32)],
        compiler_params=_cparams(2),
        name="gdn_local",
    )(qkv_a, qkv_a, conv_w, small_c.reshape(bsz, seq, N_SMALL), small_ch)


def _gdn_scan_kernel(u0_ref, w_ref, qd_ref, kt_ref, qk_ref, dec_ref, z_ref, nw_ref, y_ref, state_ref):
    i = pl.program_id(0)
    bsz, ts, bw = u0_ref.shape
    nh = bw // HEAD_DIM

    @pl.when(i == 0)
    def _():
        state_ref[...] = jnp.zeros_like(state_ref)

    nw = nw_ref[...]

    def chunk_body(ci, _):
        rows = pl.ds(pl.multiple_of(ci * CHUNK, CHUNK), CHUNK)
        units = []
        for b in range(bsz):
            for h in range(nh):
                sl = slice(h * HEAD_DIM, (h + 1) * HEAD_DIM)
                state = state_ref[b * nh + h]
                lhs = jnp.concatenate([w_ref[b, rows, sl], qd_ref[b, rows, sl]], axis=0)
                units.append(dict(b=b, h=h, sl=sl, state=state, ws=_dot(lhs, state.astype(BF16))))
        for u in units:
            b, h, sl = u["b"], u["h"], u["sl"]
            ub = (u0_ref[b, rows, sl] - u["ws"][:CHUNK]).astype(BF16)
            u["o"] = u["ws"][CHUNK:] + _dot(qk_ref[b, rows, h * CHUNK:(h + 1) * CHUNK], ub)
            u["ds"] = _dot_tn(kt_ref[b, rows, sl], ub)
        for u in units:
            b, h, sl = u["b"], u["h"], u["sl"]
            state_ref[b * nh + h] = u["state"] * dec_ref[b, ci, :, sl] + u["ds"]
            o = u["o"]
            o = o * lax.rsqrt(jnp.mean(o * o, axis=-1, keepdims=True) + NORM_EPS)
            y_ref[b, rows, sl] = (o * nw * z_ref[b, rows, sl].astype(F32)).astype(BF16)
        return 0

    lax.fori_loop(0, ts // CHUNK, chunk_body, 0)


def _gdn_scan(u0, w, qd, kt, qk, dec, z_all, o_norm_w, ts):
    bsz, seq, bw = u0.shape
    nh = bw // HEAD_DIM
    big = lambda i: (0, i, 0)
    return pl.pallas_call(
        _gdn_scan_kernel,
        out_shape=jax.ShapeDtypeStruct((bsz, seq, bw), BF16),
        grid=(seq // ts,),
        in_specs=[pl.BlockSpec((bsz, ts, bw), big),
                  pl.BlockSpec((bsz, ts, bw), big),
                  pl.BlockSpec((bsz, ts, bw), big),
                  pl.BlockSpec((bsz, ts, bw), big),
                  pl.BlockSpec((bsz, ts, nh * CHUNK), big),
                  pl.BlockSpec((bsz, ts // CHUNK, 1, bw), lambda i: (0, i, 0, 0)),
                  pl.BlockSpec((bsz, ts, bw), big),
                  pl.BlockSpec((1, HEAD_DIM), lambda i: (0, 0))],
        out_specs=pl.BlockSpec((bsz, ts, bw), big),
        scratch_shapes=[pltpu.VMEM((bsz * nh, HEAD_DIM, HEAD_DIM), F32)],
        compiler_params=_cparams(1),
        name="gdn_scan",
    )(u0, w, qd, kt, qk, dec, z_all, o_norm_w.reshape(1, HEAD_DIM))


def _conformer_kernel(u_ref, halo_ref, z_ref, cw_ref, cb_ref, lw_ref, lb_ref, y_ref, xx_ref):
    i = pl.program_id(1)
    ts = u_ref.shape[1]
    pad = halo_ref.shape[1]
    xx_ref[0:pad, :] = jnp.where(i > 0, halo_ref[0], 0.0)
    xx_ref[pad:, :] = u_ref[0]
    cw = cw_ref[...]
    off = pad - (CONF_CONV - 1)
    acc = None
    for r in range(8):
        rows = ts if r == 0 else ts + 8
        y = None
        for m in range((pad + 8) // 8):
            k = 8 * m + r - off
            if 0 <= k < CONF_CONV:
                term = cw[k:k + 1, :] * xx_ref[8 * m:8 * m + rows, :]
                y = term if y is None else y + term
        shifted = y[r:r + ts, :]
        acc = shifted if acc is None else acc + shifted
    u = acc + cb_ref[...]
    xc = u - jnp.mean(u, axis=-1, keepdims=True)
    y = xc * lax.rsqrt(jnp.mean(xc * xc, axis=-1, keepdims=True) + NORM_EPS)
    y = _silu(y * lw_ref[...] + lb_ref[...])
    y_ref[0] = (y * z_ref[0].astype(F32)).astype(BF16)


def _conformer(u, z_all, conv_w, conv_b, ln_w, ln_b, ts):
    bsz, seq, bw = u.shape
    pad = 32
    big = lambda b, i: (b, i, 0)
    vec = lambda b, i: (0, 0)
    return pl.pallas_call(
        _conformer_kernel,
        out_shape=jax.ShapeDtypeStruct((bsz, seq, bw), BF16),
        grid=(bsz, seq // ts),
        in_specs=[pl.BlockSpec((1, ts, bw), big),
                  pl.BlockSpec((1, pad, bw), lambda b, i: (b, jnp.maximum(i * (ts // pad) - 1, 0), 0)),
                  pl.BlockSpec((1, ts, bw), lambda b, i: (b, i, 1)),
                  pl.BlockSpec((CONF_CONV, bw), vec),
                  pl.BlockSpec((1, bw), vec),
                  pl.BlockSpec((1, bw), vec),
                  pl.BlockSpec((1, bw), vec)],
        out_specs=pl.BlockSpec((1, ts, bw), big),
        scratch_shapes=[pltpu.VMEM((ts + pad, bw), F32)],
        compiler_params=_cparams(2),
        name="conformer",
    )(u, u, z_all, conv_w, conv_b.reshape(1, bw), ln_w.reshape(1, bw), ln_b.reshape(1, bw))


FOX_SUB = 256
FOX_HEADS = 2


def _fox_kernel(qi_ref, kj_ref, qt_ref, k_ref, vt_ref, c_ref, z_ref, y_ref, m_ref, l_ref, acc_ref):
    t = pl.program_id(2)
    i = qi_ref[t]
    j = kj_ref[t]
    tq = qt_ref.shape[2]
    tk = k_ref.shape[1]
    hp = c_ref.shape[1]
    nsub = tq // FOX_SUB
    streams = [(g, n) for g in range(hp) for n in range(nsub)]

    @pl.when(j == 0)
    def _():
        m_ref[...] = jnp.full_like(m_ref, -jnp.inf)
        l_ref[...] = jnp.zeros_like(l_ref)
        acc_ref[...] = jnp.zeros_like(acc_ref)

    gate_rows = lax.broadcasted_iota(jnp.int32, (c_ref.shape[3], FOX_SUB), 0)
    minus_gate = jnp.where(gate_rows < N_SPLIT, -1.0, 0.0).astype(BF16)

    def step(diag):
        rows = [min((n + 1) * FOX_SUB, tk) if diag else tk for n in range(nsub)]
        s_all = []
        for g, n in streams:
            hd = slice(g * HEAD_DIM, (g + 1) * HEAD_DIM)
            k_aug = jnp.concatenate([k_ref[0, :rows[n], hd], c_ref[0, g, :rows[n], :]], axis=1)
            q_aug = jnp.concatenate([qt_ref[0, hd, n * FOX_SUB:(n + 1) * FOX_SUB], minus_gate], axis=0)
            s_all.append(_dot(k_aug, q_aug))
        for (g, n), s in zip(streams, s_all):
            hd = slice(g * HEAD_DIM, (g + 1) * HEAD_DIM)
            cols = slice(n * FOX_SUB, (n + 1) * FOX_SUB)
            if diag:
                kpos = lax.broadcasted_iota(jnp.int32, s.shape, 0)
                qpos = lax.broadcasted_iota(jnp.int32, s.shape, 1) + n * FOX_SUB
                s = jnp.where(kpos <= qpos, s, -jnp.inf)
            m_prev = m_ref[g:g + 1, cols]
            m_new = jnp.maximum(m_prev, jnp.max(s, axis=0, keepdims=True))
            p = jnp.exp2(s - m_new)
            alpha = jnp.exp2(m_prev - m_new)
            l_ref[g:g + 1, cols] = alpha * l_ref[g:g + 1, cols] + jnp.sum(p, axis=0, keepdims=True)
            acc_ref[hd, cols] = alpha * acc_ref[hd, cols] + _dot(vt_ref[0, hd, :rows[n]], p.astype(BF16))
            m_ref[g:g + 1, cols] = m_new

    @pl.when(j < i)
    def _():
        step(False)

    @pl.when(j == i)
    def _():
        step(True)
        for g in range(hp):
            hd = slice(g * HEAD_DIM, (g + 1) * HEAD_DIM)
            o = (acc_ref[hd, :] / l_ref[g:g + 1, :]).T
            y_ref[0, :, hd] = (o * z_ref[0, :, hd].astype(F32)).astype(BF16)


def _fox_attention(q_t, k, v_t, c_split, z_all, t):
    bsz, bw, seq = q_t.shape
    nh = bw // HEAD_DIM
    hp = FOX_HEADS if nh % FOX_HEADS == 0 else 1
    hw = hp * HEAD_DIM
    nt = seq // t
    steps = [(i, j) for i in range(nt) for j in range(i + 1)]
    qi = jnp.asarray([s[0] for s in steps], jnp.int32)
    kj = jnp.asarray([s[1] for s in steps], jnp.int32)
    grid_spec = pltpu.PrefetchScalarGridSpec(
        num_scalar_prefetch=2,
        grid=(bsz, nh // hp, len(steps)),
        in_specs=[pl.BlockSpec((1, hw, t), lambda b, h, s, qi, kj: (b, h, qi[s])),
                  pl.BlockSpec((1, t, hw), lambda b, h, s, qi, kj: (b, kj[s], h)),
                  pl.BlockSpec((1, hw, t), lambda b, h, s, qi, kj: (b, h, kj[s])),
                  pl.BlockSpec((1, hp, t, 128), lambda b, h, s, qi, kj: (b, h, kj[s], 0)),
                  pl.BlockSpec((1, t, hw), lambda b, h, s, qi, kj: (b, qi[s], 2 * (nh // hp) + h))],
        out_specs=pl.BlockSpec((1, t, hw), lambda b, h, s, qi, kj: (b, qi[s], h)),
        scratch_shapes=[pltpu.VMEM((hp, t), F32), pltpu.VMEM((hp, t), F32), pltpu.VMEM((hw, t), F32)],
    )
    return pl.pallas_call(
        _fox_kernel,
        out_shape=jax.ShapeDtypeStruct((bsz, seq, bw), BF16),
        grid_spec=grid_spec,
        compiler_params=_cparams(3),
        name="fox_attention",
    )(qi, kj, q_t, k, v_t, c_split, z_all)


def _merge_kernel(h_ref, ya_ref, yb_ref, yc_ref, wg0_ref, wg1_ref, wg2_ref, wb0_ref, wb1_ref, wb2_ref, o_ref):
    h = h_ref[...]
    acc = None
    for y_ref, wg_ref, wb_ref in ((ya_ref, wg0_ref, wb0_ref), (yb_ref, wg1_ref, wb1_ref), (yc_ref, wg2_ref, wb2_ref)):
        term = _sigmoid(_dot(h, wg_ref[...])) * _dot(y_ref[...], wb_ref[0])
        acc = term if acc is None else acc + term
    o_ref[...] = acc.astype(BF16)


def _merge(h, ya, yb, yc, w_packed, gate_col0, w_branch, tm, tn):
    m, d = h.shape
    bw = ya.shape[1]
    nj = d // tn
    g0 = gate_col0 // tn
    row = lambda j, i: (i, 0)
    return pl.pallas_call(
        _merge_kernel,
        out_shape=jax.ShapeDtypeStruct((m, d), BF16),
        grid=(nj, m // tm),
        in_specs=[pl.BlockSpec((tm, d), row),
                  pl.BlockSpec((tm, bw), row), pl.BlockSpec((tm, bw), row), pl.BlockSpec((tm, bw), row),
                  pl.BlockSpec((d, tn), lambda j, i: (0, g0 + j)),
                  pl.BlockSpec((d, tn), lambda j, i: (0, g0 + nj + j)),
                  pl.BlockSpec((d, tn), lambda j, i: (0, g0 + 2 * nj + j)),
                  pl.BlockSpec((1, bw, tn), lambda j, i: (0, 0, j)),
                  pl.BlockSpec((1, bw, tn), lambda j, i: (1, 0, j)),
                  pl.BlockSpec((1, bw, tn), lambda j, i: (2, 0, j))],
        out_specs=pl.BlockSpec((tm, tn), lambda j, i: (i, j)),
        compiler_params=_cparams(2),
        name="gated_merge",
    )(h, ya, yb, yc, w_packed, w_packed, w_packed, w_branch, w_branch, w_branch)


def _out_kernel(m_ref, w_ref, x_ref, pw_ref, o_ref):
    out = _dot(m_ref[...], w_ref[...])
    y = out * lax.rsqrt(jnp.mean(out * out, axis=-1, keepdims=True) + NORM_EPS)
    o_ref[...] = x_ref[...] + y * pw_ref[...]


def _out_proj(merged, w_out, x2, post_w, tm):
    m, d = x2.shape
    return pl.pallas_call(
        _out_kernel,
        out_shape=jax.ShapeDtypeStruct((m, d), F32),
        grid=(m // tm,),
        in_specs=[pl.BlockSpec((tm, d), lambda i: (i, 0)),
                  pl.BlockSpec((d, d), lambda i: (0, 0)),
                  pl.BlockSpec((tm, d), lambda i: (i, 0)),
                  pl.BlockSpec((1, d), lambda i: (0, 0))],
        out_specs=pl.BlockSpec((tm, d), lambda i: (i, 0)),
        compiler_params=_cparams(1),
        name="out_proj",
    )(merged, w_out, x2, post_w.reshape(1, d))


def _layer(x2, bsz, seq, p):
    m, d = x2.shape
    bw = d // 2
    tm = min(1024, m)
    h = _prenorm(x2, p["pre_w"], min(512, m))

    wp = p["w_packed"]
    col = _packed_columns(d)
    qkv_a = _proj(h, wp, (col["qkv_a"],), 3 * bw, epilogue="f32", tm=tm, tn=1024, out_dtype=F32)
    z_all = _proj(h, wp, (col["z_a"], col["z_b"], col["z_c"]), bw, epilogue="silu_bf16", tm=tm, tn=1024,
                  out_dtype=BF16)
    u_b = _proj_glu(h, wp, col["glu"], bw, tm=tm, tn=512)
    q_t = _proj_t(h, p["w_q_t"], bsz=bsz, tm=min(1024, seq), tn=1024, scale=HEAD_DIM ** -0.5 * LOG2_E)
    k_c = _proj(h, wp, (col["k_c"],), bw, epilogue="bf16", tm=tm, tn=1024, out_dtype=BF16)
    v_t = _proj_t(h, p["w_v_t"], bsz=bsz, tm=min(1024, seq), tn=1024)
    small_c, small_ch = _small_proj(h, p["w_small"], p["prm_c"], min(512, seq))

    z3 = z_all.reshape(bsz, seq, 3 * bw)
    u0, w, qd, kt, qk, dec = _gdn_local(qkv_a.reshape(bsz, seq, 3 * bw), p["conv_qkv_w"], small_c, small_ch,
                                        min(256, seq))
    y_a = _gdn_scan(u0, w, qd, kt, qk, dec, z3, p["o_norm_w"], min(512, seq))

    y_b = _conformer(u_b.reshape(bsz, seq, bw), z3, p["conv_w"], p["conv_b"], p["ln_w"], p["ln_b"], min(256, seq))

    c_split = _forget_cumsum(small_c.reshape(bsz, seq, N_SMALL), bw // HEAD_DIM, min(1024, seq))
    y_c = _fox_attention(q_t, k_c.reshape(bsz, seq, bw), v_t, c_split, z3, min(1024, seq))

    merged = _merge(h, y_a.reshape(m, bw), y_b.reshape(m, bw), y_c.reshape(m, bw),
                    wp, col["gate"], p["w_branch"], min(512, m), 512)
    return _out_proj(merged, p["w_out"], x2, p["post_w"], min(256, m))


REPACK_TN = 512
LANES = 128


def _packed_columns(d):
    bw = d // 2
    names = (("qkv_a", 3 * bw), ("z_a", bw), ("glu", 2 * bw), ("z_b", bw), ("q_c", bw), ("k_c", bw),
             ("v_c", bw), ("z_c", bw), ("gate", N_BRANCH * d))
    col, off = {}, 0
    for name, width in names:
        col[name] = off
        off += width
    col["total"] = off
    return col


def _repack_kernel(a_ref, b_ref, o_ref, *, tiles_by_shift):
    t = pl.program_id(1)
    tn = a_ref.shape[2]
    lo = 0
    for shift, hi in tiles_by_shift:
        @pl.when((t >= lo) & (t < hi))
        def _(shift=shift):
            if shift == 0:
                o_ref[0] = a_ref[0].astype(BF16)
            else:
                both = jnp.concatenate([a_ref[0], b_ref[0]], axis=1)
                o_ref[0] = both[:, shift:shift + tn].astype(BF16)
        lo = hi


def _repack_w_in(w_in, nh):
    depth, d, n_in = w_in.shape
    col = _packed_columns(d)
    tn = REPACK_TN
    tiles_by_shift = ((0, col["glu"] // tn), (2 * nh, col["gate"] // tn), (3 * nh, col["total"] // tn))
    assert col["glu"] % tn == 0 and col["gate"] % tn == 0 and col["total"] % tn == 0
    assert col["total"] + 3 * nh == n_in
    return pl.pallas_call(
        functools.partial(_repack_kernel, tiles_by_shift=tiles_by_shift),
        out_shape=jax.ShapeDtypeStruct((depth, d, col["total"]), BF16),
        grid=(depth, col["total"] // tn),
        in_specs=[pl.BlockSpec((1, d, tn), lambda l, t: (l, 0, t)),
                  pl.BlockSpec((1, d, LANES), lambda l, t: (l, 0, (t + 1) * (tn // LANES)))],
        out_specs=pl.BlockSpec((1, d, tn), lambda l, t: (l, 0, t)),
        compiler_params=_cparams(2),
        name="repack_w_in",
    )(w_in, w_in)


def _prep_params(pre_norm_w, post_norm_w, w_in, conv_qkv_w, a_log, dt_bias, o_norm_w,
                 conv_w, conv_b, ln_w, ln_b, f_bias, w_branch, w_out):
    depth, d, _ = w_in.shape
    bw = d // 2
    nh = bw // HEAD_DIM
    sizes = (3 * bw, bw, nh, nh, 2 * bw, bw, 3 * bw, bw, nh, N_BRANCH * d)
    offs = [0]
    for s in sizes:
        offs.append(offs[-1] + s)
    seg = lambda k: w_in[:, :, offs[k]:offs[k + 1]]
    w_packed = _repack_w_in(w_in, nh)
    col = _packed_columns(d)
    w_small = jnp.pad(jnp.concatenate([seg(2), seg(3), seg(8)], axis=2),
                      ((0, 0), (0, 0), (0, N_SMALL - 3 * nh))).astype(BF16)
    zeros = jnp.zeros((depth, nh), F32)
    ones = jnp.ones((depth, nh), F32)
    vec24 = lambda a, b, c: jnp.concatenate([a, b, c], axis=1)
    prm = jnp.stack([vec24(ones, jnp.exp(a_log.astype(F32)), ones),
                     vec24(zeros, dt_bias.astype(F32), zeros),
                     vec24(zeros, zeros, f_bias.astype(F32))], axis=1)
    prm_c = jnp.pad(prm, ((0, 0), (0, 8 - 3), (0, N_SMALL - 3 * nh)))
    return {
        "pre_w": pre_norm_w, "post_w": post_norm_w,
        "w_packed": w_packed, "w_small": w_small, "prm_c": prm_c,
        "w_q_t": jnp.swapaxes(w_packed[:, :, col["q_c"]:col["q_c"] + bw], 1, 2),
        "w_v_t": jnp.swapaxes(w_packed[:, :, col["v_c"]:col["v_c"] + bw], 1, 2),
        "conv_qkv_w": conv_qkv_w, "o_norm_w": o_norm_w,
        "conv_w": conv_w, "conv_b": conv_b, "ln_w": ln_w, "ln_b": ln_b,
        "w_branch": w_branch.astype(BF16), "w_out": w_out.astype(BF16),
    }


def kernel(x, pre_norm_w, post_norm_w, w_in, conv_qkv_w, a_log, dt_bias, o_norm_w, conv_w, conv_b, ln_w, ln_b, f_bias, w_branch, w_out):
    bsz, seq, d = x.shape
    depth = w_in.shape[0]
    params = _prep_params(pre_norm_w, post_norm_w, w_in, conv_qkv_w, a_log, dt_bias, o_norm_w,
                          conv_w, conv_b, ln_w, ln_b, f_bias, w_branch, w_out)
    x2 = x.reshape(bsz * seq, d)
    for l in range(depth):
        x2 = _layer(x2, bsz, seq, {k: v[l] for k, v in params.items()})
    return x2.reshape(bsz, seq, d)
```

```python
import functools

import jax
import jax.numpy as jnp
from jax import lax
from jax.experimental import pallas as pl
from jax.experimental.pallas import tpu as pltpu

F32 = jnp.float32
BF16 = jnp.bfloat16

HEAD_DIM = 128
CHUNK = 64
SHORT_CONV = 4
CONF_CONV = 31
NORM_EPS = 1e-6
N_BRANCH = 3
N_SMALL = 128
N_SMALL_T = 32
V7X_VMEM_LIMIT = 48 * 1024 * 1024


def _cparams(n_axes):
    return pltpu.CompilerParams(dimension_semantics=("arbitrary",) * n_axes,
                                vmem_limit_bytes=V7X_VMEM_LIMIT)


def _sigmoid(x):
    return 1.0 / (1.0 + jnp.exp(-x))


def _silu(x):
    return x * _sigmoid(x)


def _softplus(x):
    return jnp.maximum(x, 0.0) + jnp.log(1.0 + jnp.exp(-jnp.abs(x)))


def _dot(a, b):
    return jnp.dot(a, b, preferred_element_type=F32)


def _dot_nt(a, b):
    return lax.dot_general(a, b, (((1,), (1,)), ((), ())), preferred_element_type=F32)


def _dot_tn(a, b):
    return lax.dot_general(a, b, (((0,), (0,)), ((), ())), preferred_element_type=F32)


def _dot_f32(a, b):
    return jnp.dot(a, b, preferred_element_type=F32, precision=lax.Precision.HIGHEST)


def _prenorm_kernel(x_ref, w_ref, h_ref):
    x = x_ref[...]
    y = x * lax.rsqrt(jnp.mean(x * x, axis=-1, keepdims=True) + NORM_EPS)
    h_ref[...] = (y * w_ref[...]).astype(BF16)


def _prenorm(x2, w, tm):
    m, d = x2.shape
    return pl.pallas_call(
        _prenorm_kernel,
        out_shape=jax.ShapeDtypeStruct((m, d), BF16),
        grid=(m // tm,),
        in_specs=[pl.BlockSpec((tm, d), lambda i: (i, 0)),
                  pl.BlockSpec((1, d), lambda i: (0, 0))],
        out_specs=pl.BlockSpec((tm, d), lambda i: (i, 0)),
        compiler_params=_cparams(1),
        name="prenorm",
    )(x2, w.reshape(1, d))


def _proj_kernel(h_ref, w_ref, o_ref, *, epilogue):
    acc = _dot_nt(h_ref[...], w_ref[0])
    if epilogue == "f32":
        o_ref[...] = acc
    elif epilogue == "silu_bf16":
        o_ref[...] = _silu(acc).astype(BF16)
    elif epilogue == "bf16":
        o_ref[...] = acc.astype(BF16)
    else:
        raise ValueError(epilogue)


def _proj(h, w_packed, layer, seg_cols, seg_width, *, epilogue, tm, tn, out_dtype):
    m, k = h.shape
    n = len(seg_cols) * seg_width
    per_seg = seg_width // tn

    def w_tile(j):
        tile = seg_cols[0] // tn + j
        for s in range(1, len(seg_cols)):
            tile = jnp.where(j >= s * per_seg, seg_cols[s] // tn + j - s * per_seg, tile)
        return tile
    return pl.pallas_call(
        functools.partial(_proj_kernel, epilogue=epilogue),
        out_shape=jax.ShapeDtypeStruct((m, n), out_dtype),
        grid=(n // tn, m // tm),
        in_specs=[pl.BlockSpec((tm, k), lambda j, i: (i, 0)),
                  pl.BlockSpec((1, tn, k), lambda j, i: (layer, w_tile(j), 0))],
        out_specs=pl.BlockSpec((tm, tn), lambda j, i: (i, j)),
        compiler_params=_cparams(2),
        name="proj_" + epilogue,
    )(h, w_packed)


def _proj_t_kernel(wt_ref, h_ref, o_ref, *, scale):
    o_ref[0] = (_dot_nt(wt_ref[0], h_ref[...]) * scale).astype(BF16)


def _proj_t(h, w_packed, layer, row0, n, *, bsz, tm, tn, scale=1.0):
    m, k = h.shape
    seq = m // bsz
    nts = seq // tm
    return pl.pallas_call(
        functools.partial(_proj_t_kernel, scale=scale),
        out_shape=jax.ShapeDtypeStruct((bsz, n, seq), BF16),
        grid=(n // tn, m // tm),
        in_specs=[pl.BlockSpec((1, tn, k), lambda j, i: (layer, row0 // tn + j, 0)),
                  pl.BlockSpec((tm, k), lambda j, i: (i, 0))],
        out_specs=pl.BlockSpec((1, tn, tm), lambda j, i: (i // nts, j, i % nts)),
        compiler_params=_cparams(2),
        name="proj_t",
    )(w_packed, h)


def _glu_kernel(h_ref, wv_ref, wg_ref, o_ref):
    h = h_ref[...]
    o_ref[...] = _dot_nt(h, wv_ref[0]) * _sigmoid(_dot_nt(h, wg_ref[0]))


def _proj_glu(h, w_packed, layer, col0, n, *, tm, tn):
    m, k = h.shape
    return pl.pallas_call(
        _glu_kernel,
        out_shape=jax.ShapeDtypeStruct((m, n), F32),
        grid=(n // tn, m // tm),
        in_specs=[pl.BlockSpec((tm, k), lambda j, i: (i, 0)),
                  pl.BlockSpec((1, tn, k), lambda j, i: (layer, col0 // tn + j, 0)),
                  pl.BlockSpec((1, tn, k), lambda j, i: (layer, (col0 + n) // tn + j, 0))],
        out_specs=pl.BlockSpec((tm, tn), lambda j, i: (i, j)),
        compiler_params=_cparams(2),
        name="proj_glu",
    )(h, w_packed, w_packed)


def _gate_transform(x, exp_a, dt_bias, f_bias, idx):
    beta = _sigmoid(x)
    log_a = -exp_a * _softplus(x + dt_bias)
    log_f = -_softplus(-(x + f_bias))
    return jnp.where(idx < 8, beta, jnp.where(idx < 16, log_a, log_f))


def _small_kernel(h_ref, w_ref, pc_ref, o_ref, oc_ref):
    col = _dot_nt(h_ref[...], w_ref[0].astype(BF16))
    pc = pc_ref[0]
    cidx = lax.broadcasted_iota(jnp.int32, col.shape, 1)
    col = _gate_transform(col, pc[0:1, :], pc[1:2, :], pc[2:3, :], cidx)
    o_ref[...] = col
    row = col.T
    for c in range(row.shape[1] // CHUNK):
        oc_ref[c] = row[:N_SMALL_T, c * CHUNK:(c + 1) * CHUNK]


def _small_proj(h, w_small, prm_c, layer, tm):
    m, k = h.shape
    return pl.pallas_call(
        _small_kernel,
        out_shape=(jax.ShapeDtypeStruct((m, N_SMALL), F32),
                   jax.ShapeDtypeStruct((m // CHUNK, N_SMALL_T, CHUNK), F32)),
        grid=(m // tm,),
        in_specs=[pl.BlockSpec((tm, k), lambda i: (i, 0)),
                  pl.BlockSpec((1, N_SMALL, k), lambda i: (layer, 0, 0)),
                  pl.BlockSpec((1, 8, N_SMALL), lambda i: (layer, 0, 0))],
        out_specs=(pl.BlockSpec((tm, N_SMALL), lambda i: (i, 0)),
                   pl.BlockSpec((tm // CHUNK, N_SMALL_T, CHUNK), lambda i: (i, 0, 0))),
        compiler_params=_cparams(1),
        name="small_proj",
    )(h, w_small, prm_c)


def _split3(x):
    a = x.astype(BF16)
    r1 = x - a.astype(F32)
    b = r1.astype(BF16)
    c = (r1 - b.astype(F32)).astype(BF16)
    return a, b, c


LOG2_E = 1.4426950408889634
N_SPLIT = 3


def _cumsum_kernel(x_ref, o_ref, carry_ref):
    ts = x_ref.shape[1]
    nh = o_ref.shape[1]
    blk = 128

    @pl.when(pl.program_id(1) == 0)
    def _():
        carry_ref[...] = jnp.zeros_like(carry_ref)

    r = lax.broadcasted_iota(jnp.int32, (blk, blk), 0)
    c = lax.broadcasted_iota(jnp.int32, (blk, blk), 1)
    tril = jnp.where(r >= c, 1.0, 0.0).astype(BF16)
    er = lax.broadcasted_iota(jnp.int32, (blk, nh * blk), 0)
    ec = lax.broadcasted_iota(jnp.int32, (blk, nh * blk), 1)
    place = [jnp.where((er == 2 * nh + ec // blk) & (ec % blk == t), 1.0, 0.0).astype(BF16)
             for t in range(N_SPLIT)]

    carry = carry_ref[0:1, :]
    for j in range(ts // blk):
        rows = slice(j * blk, (j + 1) * blk)
        x1, x2, x3 = _split3(x_ref[0, rows, :])
        run = _dot(tril, x1) + _dot(tril, x2) + _dot(tril, x3) + carry
        carry = run[blk - 1:blk, :]
        parts = _split3(run * LOG2_E)
        rep = _dot(parts[0], place[0]) + _dot(parts[1], place[1]) + _dot(parts[2], place[2])
        for h in range(nh):
            o_ref[0, h, rows, :] = rep[:, h * blk:(h + 1) * blk].astype(BF16)
    carry_ref[0:1, :] = carry


def _forget_cumsum(small_c3, nh, ts):
    bsz, seq, _ = small_c3.shape
    return pl.pallas_call(
        _cumsum_kernel,
        out_shape=jax.ShapeDtypeStruct((bsz, nh, seq, 128), BF16),
        grid=(bsz, seq // ts),
        in_specs=[pl.BlockSpec((1, ts, N_SMALL), lambda b, i: (b, i, 0))],
        out_specs=pl.BlockSpec((1, nh, ts, 128), lambda b, i: (b, 0, i, 0)),
        scratch_shapes=[pltpu.VMEM((8, N_SMALL), F32)],
        compiler_params=_cparams(2),
        name="forget_cumsum",
    )(small_c3)


GDN_CHUNKS_PER_ITER = 2


def _gdn_local_kernel(qkv_ref, halo_ref, cw_ref, sc_ref, sr_ref,
                      u0_ref, w_ref, qd_ref, kt_ref, qk_ref, dec_ref, xx_ref, act_ref):
    i = pl.program_id(1)
    ts = qkv_ref.shape[1]
    bw = u0_ref.shape[2]
    nh = bw // HEAD_DIM
    halo = halo_ref[0]
    xx_ref[0:8, :] = jnp.where(i > 0, halo, 0.0)
    xx_ref[8:, :] = qkv_ref[0]

    cw = cw_ref[...]
    for rb in range(ts // CHUNK):
        acc = None
        for k in range(SHORT_CONV):
            start = rb * CHUNK + (8 - (SHORT_CONV - 1)) + k
            term = cw[k:k + 1, :] * xx_ref[start:start + CHUNK, :]
            acc = term if acc is None else acc + term
        act_ref[rb * CHUNK:(rb + 1) * CHUNK, :] = _silu(acc)

    r = lax.broadcasted_iota(jnp.int32, (CHUNK, CHUNK), 0)
    c = lax.broadcasted_iota(jnp.int32, (CHUNK, CHUNK), 1)
    incl = r >= c
    strict = r > c
    tril = incl.astype(F32)
    triu = (r <= c).astype(F32)
    eye = jnp.where(r == c, 1.0, 0.0).astype(F32)
    nchunks = ts // CHUNK
    cpi = GDN_CHUNKS_PER_ITER if nchunks % GDN_CHUNKS_PER_ITER == 0 else 1

    def iter_body(it, _):
        units = []
        for cc in range(cpi):
            ci = it * cpi + cc
            base = pl.multiple_of(ci * CHUNK, CHUNK)
            rows = pl.ds(base, CHUNK)
            sc = sc_ref[0, rows, :]
            g_col = _dot_f32(tril, sc)
            g_row = _dot_f32(sr_ref[ci], triu)
            for h in range(nh):
                q = act_ref[rows, h * HEAD_DIM:(h + 1) * HEAD_DIM]
                k_ = act_ref[rows, bw + h * HEAD_DIM: bw + (h + 1) * HEAD_DIM]
                v = act_ref[rows, 2 * bw + h * HEAD_DIM: 2 * bw + (h + 1) * HEAD_DIM]
                q = q * (lax.rsqrt(jnp.sum(q * q, axis=-1, keepdims=True) + NORM_EPS) * HEAD_DIM ** -0.5)
                k_ = k_ * lax.rsqrt(jnp.sum(k_ * k_, axis=-1, keepdims=True) + NORM_EPS)
                beta = sc[:, h:h + 1]
                g = g_col[:, nh + h:nh + h + 1]
                g_last = g_col[CHUNK - 1:CHUNK, nh + h:nh + h + 1]
                gr = g_row[nh + h:nh + h + 1, :]
                units.append(dict(
                    ci=ci, rows=rows, h=h, q=q, k=k_, v=v, beta=beta, g=g, g_last=g_last,
                    decay=jnp.exp(jnp.where(incl, g - gr, -jnp.inf)),
                    eg=jnp.exp(g), kb=k_.astype(BF16), qb=q.astype(BF16)))

        for u in units:
            u["kq"] = _dot_nt(jnp.concatenate([u["kb"], u["qb"]], axis=0), u["kb"])
        for u in units:
            u["qk"] = u["kq"][CHUNK:] * u["decay"]
            a = jnp.where(strict, -(u["beta"] * u["kq"][:CHUNK] * u["decay"]), 0.0)
            u["x"] = eye + a
            u["p"] = a
        for u in units:
            pb = u["p"].astype(BF16)
            u["p"] = _dot(pb, pb)
        m = 2
        while 2 * m < CHUNK:
            for u in units:
                pb = u["p"].astype(BF16)
                out = _dot(jnp.concatenate([u["x"].astype(BF16), pb], axis=0), pb)
                u["x"] = u["x"] + out[:CHUNK]
                u["p"] = out[CHUNK:]
            m *= 2
        for u in units:
            u["x"] = u["x"] + _dot(u["x"].astype(BF16), u["p"].astype(BF16))
        for u in units:
            rhs = jnp.concatenate([u["v"] * u["beta"], u["k"] * (u["beta"] * u["eg"])], axis=1)
            u["uw"] = _dot(u["x"].astype(BF16), rhs.astype(BF16))

        for u in units:
            h, rows = u["h"], u["rows"]
            sl = slice(h * HEAD_DIM, (h + 1) * HEAD_DIM)
            u0_ref[0, rows, sl] = u["uw"][:, :HEAD_DIM]
            w_ref[0, rows, sl] = u["uw"][:, HEAD_DIM:].astype(BF16)
            qd_ref[0, rows, sl] = (u["q"] * u["eg"]).astype(BF16)
            kt_ref[0, rows, sl] = (u["k"] * jnp.exp(u["g_last"] - u["g"])).astype(BF16)
            qk_ref[0, rows, h * CHUNK:(h + 1) * CHUNK] = u["qk"].astype(BF16)
            dec_ref[0, u["ci"], :, sl] = jnp.broadcast_to(jnp.exp(u["g_last"]), (1, HEAD_DIM))
        return 0

    lax.fori_loop(0, nchunks // cpi, iter_body, 0)


def _gdn_local(qkv_a, conv_w, small_c, small_ch, ts):
    bsz, seq, n3 = qkv_a.shape
    bw = n3 // 3
    nh = bw // HEAD_DIM
    nt = seq // ts
    nc = ts // CHUNK
    big = lambda b, i: (b, i, 0)
    return pl.pallas_call(
        _gdn_local_kernel,
        out_shape=(jax.ShapeDtypeStruct((bsz, seq, bw), F32),
                   jax.ShapeDtypeStruct((bsz, seq, bw), BF16),
                   jax.ShapeDtypeStruct((bsz, seq, bw), BF16),
                   jax.ShapeDtypeStruct((bsz, seq, bw), BF16),
                   jax.ShapeDtypeStruct((bsz, seq, nh * CHUNK), BF16),
                   jax.ShapeDtypeStruct((bsz, seq // CHUNK, 1, bw), F32)),
        grid=(bsz, nt),
        in_specs=[pl.BlockSpec((1, ts, n3), big),
                  pl.BlockSpec((1, 8, n3), lambda b, i: (b, jnp.maximum(i * (ts // 8) - 1, 0), 0)),
                  pl.BlockSpec((SHORT_CONV, n3), lambda b, i: (0, 0)),
                  pl.BlockSpec((1, ts, N_SMALL), big),
                  pl.BlockSpec((nc, N_SMALL_T, CHUNK), lambda b, i: (b * nt + i, 0, 0))],
        out_specs=(pl.BlockSpec((1, ts, bw), big),
                   pl.BlockSpec((1, ts, bw), big),
                   pl.BlockSpec((1, ts, bw), big),
                   pl.BlockSpec((1, ts, bw), big),
                   pl.BlockSpec((1, ts, nh * CHUNK), big),
                   pl.BlockSpec((1, nc, 1, bw), lambda b, i: (b, i, 0, 0))),
        scratch_shapes=[pltpu.VMEM((ts + 8, n3), F32), pltpu.VMEM((ts, n3), F32)],
        compiler_params=_cparams(2),
        name="gdn_local",
    )(qkv_a, qkv_a, conv_w, small_c.reshape(bsz, seq, N_SMALL), small_ch)


def _gdn_scan_kernel(u0_ref, w_ref, qd_ref, kt_ref, qk_ref, dec_ref, z_ref, nw_ref, y_ref, state_ref):
    i = pl.program_id(0)
    bsz, ts, bw = u0_ref.shape
    nh = bw // HEAD_DIM

    @pl.when(i == 0)
    def _():
        state_ref[...] = jnp.zeros_like(state_ref)

    nw = nw_ref[...]

    def chunk_body(ci, _):
        rows = pl.ds(pl.multiple_of(ci * CHUNK, CHUNK), CHUNK)
        units = []
        for b in range(bsz):
            for h in range(nh):
                sl = slice(h * HEAD_DIM, (h + 1) * HEAD_DIM)
                state = state_ref[b * nh + h]
                lhs = jnp.concatenate([w_ref[b, rows, sl], qd_ref[b, rows, sl]], axis=0)
                units.append(dict(b=b, h=h, sl=sl, state=state, ws=_dot(lhs, state.astype(BF16))))
        for u in units:
            b, h, sl = u["b"], u["h"], u["sl"]
            ub = (u0_ref[b, rows, sl] - u["ws"][:CHUNK]).astype(BF16)
            u["o"] = u["ws"][CHUNK:] + _dot(qk_ref[b, rows, h * CHUNK:(h + 1) * CHUNK], ub)
            u["ds"] = _dot_tn(kt_ref[b, rows, sl], ub)
        for u in units:
            b, h, sl = u["b"], u["h"], u["sl"]
            state_ref[b * nh + h] = u["state"] * dec_ref[b, ci, :, sl] + u["ds"]
            o = u["o"]
            o = o * lax.rsqrt(jnp.mean(o * o, axis=-1, keepdims=True) + NORM_EPS)
            y_ref[b, rows, sl] = (o * nw * z_ref[b, rows, sl].astype(F32)).astype(BF16)
        return 0

    lax.fori_loop(0, ts // CHUNK, chunk_body, 0)


def _gdn_scan(u0, w, qd, kt, qk, dec, z_all, o_norm_w, ts):
    bsz, seq, bw = u0.shape
    nh = bw // HEAD_DIM
    big = lambda i: (0, i, 0)
    return pl.pallas_call(
        _gdn_scan_kernel,
        out_shape=jax.ShapeDtypeStruct((bsz, seq, bw), BF16),
        grid=(seq // ts,),
        in_specs=[pl.BlockSpec((bsz, ts, bw), big),
                  pl.BlockSpec((bsz, ts, bw), big),
                  pl.BlockSpec((bsz, ts, bw), big),
                  pl.BlockSpec((bsz, ts, bw), big),
                  pl.BlockSpec((bsz, ts, nh * CHUNK), big),
                  pl.BlockSpec((bsz, ts // CHUNK, 1, bw), lambda i: (0, i, 0, 0)),
                  pl.BlockSpec((bsz, ts, bw), big),
                  pl.BlockSpec((1, HEAD_DIM), lambda i: (0, 0))],
        out_specs=pl.BlockSpec((bsz, ts, bw), big),
        scratch_shapes=[pltpu.VMEM((bsz * nh, HEAD_DIM, HEAD_DIM), F32)],
        compiler_params=_cparams(1),
        name="gdn_scan",
    )(u0, w, qd, kt, qk, dec, z_all, o_norm_w.reshape(1, HEAD_DIM))


def _conformer_kernel(u_ref, halo_ref, z_ref, cw_ref, cb_ref, lw_ref, lb_ref, y_ref, xx_ref):
    i = pl.program_id(1)
    ts = u_ref.shape[1]
    pad = halo_ref.shape[1]
    xx_ref[0:pad, :] = jnp.where(i > 0, halo_ref[0], 0.0)
    xx_ref[pad:, :] = u_ref[0]
    cw = cw_ref[...]
    off = pad - (CONF_CONV - 1)
    acc = None
    for r in range(8):
        rows = ts if r == 0 else ts + 8
        y = None
        for m in range((pad + 8) // 8):
            k = 8 * m + r - off
            if 0 <= k < CONF_CONV:
                term = cw[k:k + 1, :] * xx_ref[8 * m:8 * m + rows, :]
                y = term if y is None else y + term
        shifted = y[r:r + ts, :]
        acc = shifted if acc is None else acc + shifted
    u = acc + cb_ref[...]
    xc = u - jnp.mean(u, axis=-1, keepdims=True)
    y = xc * lax.rsqrt(jnp.mean(xc * xc, axis=-1, keepdims=True) + NORM_EPS)
    y = _silu(y * lw_ref[...] + lb_ref[...])
    y_ref[0] = (y * z_ref[0].astype(F32)).astype(BF16)


def _conformer(u, z_all, conv_w, conv_b, ln_w, ln_b, ts):
    bsz, seq, bw = u.shape
    pad = 32
    big = lambda b, i: (b, i, 0)
    vec = lambda b, i: (0, 0)
    return pl.pallas_call(
        _conformer_kernel,
        out_shape=jax.ShapeDtypeStruct((bsz, seq, bw), BF16),
        grid=(bsz, seq // ts),
        in_specs=[pl.BlockSpec((1, ts, bw), big),
                  pl.BlockSpec((1, pad, bw), lambda b, i: (b, jnp.maximum(i * (ts // pad) - 1, 0), 0)),
                  pl.BlockSpec((1, ts, bw), lambda b, i: (b, i, 1)),
                  pl.BlockSpec((CONF_CONV, bw), vec),
                  pl.BlockSpec((1, bw), vec),
                  pl.BlockSpec((1, bw), vec),
                  pl.BlockSpec((1, bw), vec)],
        out_specs=pl.BlockSpec((1, ts, bw), big),
        scratch_shapes=[pltpu.VMEM((ts + pad, bw), F32)],
        compiler_params=_cparams(2),
        name="conformer",
    )(u, u, z_all, conv_w, conv_b.reshape(1, bw), ln_w.reshape(1, bw), ln_b.reshape(1, bw))


FOX_SUB = 256
FOX_HEADS = 2


def _fox_kernel(qi_ref, kj_ref, qt_ref, k_ref, vt_ref, c_ref, z_ref, y_ref, m_ref, l_ref, acc_ref):
    t = pl.program_id(2)
    i = qi_ref[t]
    j = kj_ref[t]
    tq = qt_ref.shape[2]
    tk = k_ref.shape[1]
    hp = c_ref.shape[1]
    nsub = tq // FOX_SUB
    streams = [(g, n) for g in range(hp) for n in range(nsub)]

    @pl.when(j == 0)
    def _():
        m_ref[...] = jnp.full_like(m_ref, -jnp.inf)
        l_ref[...] = jnp.zeros_like(l_ref)
        acc_ref[...] = jnp.zeros_like(acc_ref)

    gate_rows = lax.broadcasted_iota(jnp.int32, (c_ref.shape[3], FOX_SUB), 0)
    minus_gate = jnp.where(gate_rows < N_SPLIT, -1.0, 0.0).astype(BF16)

    def step(diag):
        rows = [min((n + 1) * FOX_SUB, tk) if diag else tk for n in range(nsub)]
        s_all = []
        for g, n in streams:
            hd = slice(g * HEAD_DIM, (g + 1) * HEAD_DIM)
            k_aug = jnp.concatenate([k_ref[0, :rows[n], hd], c_ref[0, g, :rows[n], :]], axis=1)
            q_aug = jnp.concatenate([qt_ref[0, hd, n * FOX_SUB:(n + 1) * FOX_SUB], minus_gate], axis=0)
            s_all.append(_dot(k_aug, q_aug))
        for (g, n), s in zip(streams, s_all):
            hd = slice(g * HEAD_DIM, (g + 1) * HEAD_DIM)
            cols = slice(n * FOX_SUB, (n + 1) * FOX_SUB)
            if diag:
                kpos = lax.broadcasted_iota(jnp.int32, s.shape, 0)
                qpos = lax.broadcasted_iota(jnp.int32, s.shape, 1) + n * FOX_SUB
                s = jnp.where(kpos <= qpos, s, -jnp.inf)
            m_prev = m_ref[g:g + 1, cols]
            m_new = jnp.maximum(m_prev, jnp.max(s, axis=0, keepdims=True))
            p = jnp.exp2(s - m_new)
            alpha = jnp.exp2(m_prev - m_new)
            l_ref[g:g + 1, cols] = alpha * l_ref[g:g + 1, cols] + jnp.sum(p, axis=0, keepdims=True)
            acc_ref[hd, cols] = alpha * acc_ref[hd, cols] + _dot(vt_ref[0, hd, :rows[n]], p.astype(BF16))
            m_ref[g:g + 1, cols] = m_new

    @pl.when(j < i)
    def _():
        step(False)

    @pl.when(j == i)
    def _():
        step(True)
        for g in range(hp):
            hd = slice(g * HEAD_DIM, (g + 1) * HEAD_DIM)
            o = (acc_ref[hd, :] / l_ref[g:g + 1, :]).T
            y_ref[0, :, hd] = (o * z_ref[0, :, hd].astype(F32)).astype(BF16)


def _fox_attention(q_t, k, v_t, c_split, z_all, t):
    bsz, bw, seq = q_t.shape
    nh = bw // HEAD_DIM
    hp = FOX_HEADS if nh % FOX_HEADS == 0 else 1
    hw = hp * HEAD_DIM
    nt = seq // t
    steps = [(i, j) for i in range(nt) for j in range(i + 1)]
    qi = jnp.asarray([s[0] for s in steps], jnp.int32)
    kj = jnp.asarray([s[1] for s in steps], jnp.int32)
    grid_spec = pltpu.PrefetchScalarGridSpec(
        num_scalar_prefetch=2,
        grid=(bsz, nh // hp, len(steps)),
        in_specs=[pl.BlockSpec((1, hw, t), lambda b, h, s, qi, kj: (b, h, qi[s])),
                  pl.BlockSpec((1, t, hw), lambda b, h, s, qi, kj: (b, kj[s], h)),
                  pl.BlockSpec((1, hw, t), lambda b, h, s, qi, kj: (b, h, kj[s])),
                  pl.BlockSpec((1, hp, t, 128), lambda b, h, s, qi, kj: (b, h, kj[s], 0)),
                  pl.BlockSpec((1, t, hw), lambda b, h, s, qi, kj: (b, qi[s], 2 * (nh // hp) + h))],
        out_specs=pl.BlockSpec((1, t, hw), lambda b, h, s, qi, kj: (b, qi[s], h)),
        scratch_shapes=[pltpu.VMEM((hp, t), F32), pltpu.VMEM((hp, t), F32), pltpu.VMEM((hw, t), F32)],
    )
    return pl.pallas_call(
        _fox_kernel,
        out_shape=jax.ShapeDtypeStruct((bsz, seq, bw), BF16),
        grid_spec=grid_spec,
        compiler_params=_cparams(3),
        name="fox_attention",
    )(qi, kj, q_t, k, v_t, c_split, z_all)


def _merge_kernel(h_ref, ya_ref, yb_ref, yc_ref, wg0_ref, wg1_ref, wg2_ref, wb0_ref, wb1_ref, wb2_ref, o_ref):
    h = h_ref[...]
    acc = None
    for y_ref, wg_ref, wb_ref in ((ya_ref, wg0_ref, wb0_ref), (yb_ref, wg1_ref, wb1_ref), (yc_ref, wg2_ref, wb2_ref)):
        term = _sigmoid(_dot_nt(h, wg_ref[0])) * _dot(y_ref[...], wb_ref[0, 0])
        acc = term if acc is None else acc + term
    o_ref[...] = acc.astype(BF16)


def _merge(h, ya, yb, yc, w_packed, gate_col0, w_branch, layer, tm, tn):
    m, d = h.shape
    bw = ya.shape[1]
    nj = d // tn
    g0 = gate_col0 // tn
    row = lambda j, i: (i, 0)
    return pl.pallas_call(
        _merge_kernel,
        out_shape=jax.ShapeDtypeStruct((m, d), BF16),
        grid=(nj, m // tm),
        in_specs=[pl.BlockSpec((tm, d), row),
                  pl.BlockSpec((tm, bw), row), pl.BlockSpec((tm, bw), row), pl.BlockSpec((tm, bw), row),
                  pl.BlockSpec((1, tn, d), lambda j, i: (layer, g0 + j, 0)),
                  pl.BlockSpec((1, tn, d), lambda j, i: (layer, g0 + nj + j, 0)),
                  pl.BlockSpec((1, tn, d), lambda j, i: (layer, g0 + 2 * nj + j, 0)),
                  pl.BlockSpec((1, 1, bw, tn), lambda j, i: (layer, 0, 0, j)),
                  pl.BlockSpec((1, 1, bw, tn), lambda j, i: (layer, 1, 0, j)),
                  pl.BlockSpec((1, 1, bw, tn), lambda j, i: (layer, 2, 0, j))],
        out_specs=pl.BlockSpec((tm, tn), lambda j, i: (i, j)),
        compiler_params=_cparams(2),
        name="gated_merge",
    )(h, ya, yb, yc, w_packed, w_packed, w_packed, w_branch, w_branch, w_branch)


def _out_kernel(m_ref, w_ref, x_ref, pw_ref, o_ref):
    out = _dot(m_ref[...], w_ref[0])
    y = out * lax.rsqrt(jnp.mean(out * out, axis=-1, keepdims=True) + NORM_EPS)
    o_ref[...] = x_ref[...] + y * pw_ref[...]


def _out_proj(merged, w_out, layer, x2, post_w, tm):
    m, d = x2.shape
    return pl.pallas_call(
        _out_kernel,
        out_shape=jax.ShapeDtypeStruct((m, d), F32),
        grid=(m // tm,),
        in_specs=[pl.BlockSpec((tm, d), lambda i: (i, 0)),
                  pl.BlockSpec((1, d, d), lambda i: (layer, 0, 0)),
                  pl.BlockSpec((tm, d), lambda i: (i, 0)),
                  pl.BlockSpec((1, d), lambda i: (0, 0))],
        out_specs=pl.BlockSpec((tm, d), lambda i: (i, 0)),
        compiler_params=_cparams(1),
        name="out_proj",
    )(merged, w_out, x2, post_w.reshape(1, d))


def _layer(x2, bsz, seq, l, shared, p):
    m, d = x2.shape
    bw = d // 2
    tm = min(1024, m)
    h = _prenorm(x2, p["pre_w"], min(512, m))

    wp = shared["w_packed"]
    col = _packed_columns(d)
    qkv_a = _proj(h, wp, l, (col["qkv_a"],), 3 * bw, epilogue="f32", tm=tm, tn=1024, out_dtype=F32)
    z_all = _proj(h, wp, l, (col["z_a"], col["z_b"], col["z_c"]), bw, epilogue="silu_bf16", tm=tm, tn=1024,
                  out_dtype=BF16)
    u_b = _proj_glu(h, wp, l, col["glu"], bw, tm=tm, tn=512)
    q_t = _proj_t(h, wp, l, col["q_c"], bw, bsz=bsz, tm=min(1024, seq), tn=1024,
                  scale=HEAD_DIM ** -0.5 * LOG2_E)
    k_c = _proj(h, wp, l, (col["k_c"],), bw, epilogue="bf16", tm=tm, tn=1024, out_dtype=BF16)
    v_t = _proj_t(h, wp, l, col["v_c"], bw, bsz=bsz, tm=min(1024, seq), tn=1024)
    small_c, small_ch = _small_proj(h, shared["w_small"], shared["prm_c"], l, min(512, seq))

    z3 = z_all.reshape(bsz, seq, 3 * bw)
    u0, w, qd, kt, qk, dec = _gdn_local(qkv_a.reshape(bsz, seq, 3 * bw), p["conv_qkv_w"], small_c, small_ch,
                                        min(256, seq))
    y_a = _gdn_scan(u0, w, qd, kt, qk, dec, z3, p["o_norm_w"], min(512, seq))

    y_b = _conformer(u_b.reshape(bsz, seq, bw), z3, p["conv_w"], p["conv_b"], p["ln_w"], p["ln_b"], min(256, seq))

    c_split = _forget_cumsum(small_c.reshape(bsz, seq, N_SMALL), bw // HEAD_DIM, min(1024, seq))
    y_c = _fox_attention(q_t, k_c.reshape(bsz, seq, bw), v_t, c_split, z3, min(1024, seq))

    merged = _merge(h, y_a.reshape(m, bw), y_b.reshape(m, bw), y_c.reshape(m, bw),
                    wp, col["gate"], shared["w_branch"], l, min(512, m), 512)
    return _out_proj(merged, shared["w_out"], l, x2, p["post_w"], min(256, m))


REPACK_TR = 512


def _packed_columns(d):
    bw = d // 2
    names = (("qkv_a", 3 * bw), ("z_a", bw), ("glu", 2 * bw), ("z_b", bw), ("q_c", bw), ("k_c", bw),
             ("v_c", bw), ("z_c", bw), ("gate", N_BRANCH * d))
    col, off = {}, 0
    for name, width in names:
        col[name] = off
        off += width
    col["total"] = off
    return col


def _repack_kernel(x_ref, o_ref):
    o_ref[...] = x_ref[...].astype(BF16)


def _repack_w_in(w_t, nh):
    depth, n_in, d = w_t.shape
    col = _packed_columns(d)
    tr = REPACK_TR
    t_glu, t_gate = col["glu"] // tr, col["gate"] // tr
    assert col["glu"] % tr == 0 and col["gate"] % tr == 0 and col["total"] % tr == 0
    assert col["total"] + 3 * nh == n_in

    def src_row(t):
        row = t * tr + jnp.where(t >= t_gate, 3 * nh, jnp.where(t >= t_glu, 2 * nh, 0))
        return pl.multiple_of(row, nh)
    return pl.pallas_call(
        _repack_kernel,
        out_shape=jax.ShapeDtypeStruct((depth, col["total"], d), BF16),
        grid=(depth, col["total"] // tr),
        in_specs=[pl.BlockSpec((pl.Element(1), pl.Element(tr), pl.Element(d)), lambda l, t: (l, src_row(t), 0))],
        out_specs=pl.BlockSpec((1, tr, d), lambda l, t: (l, t, 0)),
        compiler_params=_cparams(2),
        name="repack_w_in",
    )(w_t)


def _small_w_kernel(x_ref, y_ref, o_ref):
    o_ref[...] = jnp.zeros_like(o_ref)
    nx, ny = x_ref.shape[1], y_ref.shape[1]
    o_ref[0, 0:nx, :] = x_ref[0]
    o_ref[0, nx:nx + ny, :] = y_ref[0]


def _small_w(w_t, nh):
    depth, _, d = w_t.shape
    bw = d // 2
    beta0 = 4 * bw
    forget0 = beta0 + 2 * nh + 7 * bw
    return pl.pallas_call(
        _small_w_kernel,
        out_shape=jax.ShapeDtypeStruct((depth, N_SMALL, d), F32),
        grid=(depth,),
        in_specs=[pl.BlockSpec((pl.Element(1), pl.Element(2 * nh), pl.Element(d)), lambda l: (l, beta0, 0)),
                  pl.BlockSpec((pl.Element(1), pl.Element(nh), pl.Element(d)), lambda l: (l, forget0, 0))],
        out_specs=pl.BlockSpec((1, N_SMALL, d), lambda l: (l, 0, 0)),
        compiler_params=_cparams(1),
        name="small_w",
    )(w_t, w_t)


def _prep_params(pre_norm_w, post_norm_w, w_in, conv_qkv_w, a_log, dt_bias, o_norm_w,
                 conv_w, conv_b, ln_w, ln_b, f_bias, w_branch, w_out):
    depth, d, _ = w_in.shape
    bw = d // 2
    nh = bw // HEAD_DIM
    w_t = jnp.swapaxes(w_in, 1, 2)
    w_packed = _repack_w_in(w_t, nh)
    w_small = _small_w(w_t, nh)
    zeros = jnp.zeros((depth, nh), F32)
    ones = jnp.ones((depth, nh), F32)
    vec24 = lambda a, b, c: jnp.concatenate([a, b, c], axis=1)
    prm = jnp.stack([vec24(ones, jnp.exp(a_log.astype(F32)), ones),
                     vec24(zeros, dt_bias.astype(F32), zeros),
                     vec24(zeros, zeros, f_bias.astype(F32))], axis=1)
    prm_c = jnp.pad(prm, ((0, 0), (0, 8 - 3), (0, N_SMALL - 3 * nh)))
    shared = {"w_packed": w_packed, "w_small": w_small, "prm_c": prm_c,
              "w_branch": w_branch.astype(BF16), "w_out": w_out.astype(BF16)}
    per_layer = {"pre_w": pre_norm_w, "post_w": post_norm_w, "conv_qkv_w": conv_qkv_w, "o_norm_w": o_norm_w,
                 "conv_w": conv_w, "conv_b": conv_b, "ln_w": ln_w, "ln_b": ln_b}
    return shared, per_layer


def kernel(x, pre_norm_w, post_norm_w, w_in, conv_qkv_w, a_log, dt_bias, o_norm_w, conv_w, conv_b, ln_w, ln_b, f_bias, w_branch, w_out):
    bsz, seq, d = x.shape
    depth = w_in.shape[0]
    shared, per_layer = _prep_params(pre_norm_w, post_norm_w, w_in, conv_qkv_w, a_log, dt_bias, o_norm_w,
                                     conv_w, conv_b, ln_w, ln_b, f_bias, w_branch, w_out)
    x2 = x.reshape(bsz * seq, d)
    for l in range(depth):
        x2 = _layer(x2, bsz, seq, l, shared, {k: v[l] for k, v in per_layer.items()})
    return x2.reshape(bsz, seq, d)
```

```python
import functools

import jax
import jax.numpy as jnp
from jax import lax
from jax.experimental import pallas as pl
from jax.experimental.pallas import tpu as pltpu

F32 = jnp.float32
BF16 = jnp.bfloat16

HEAD_DIM = 128
CHUNK = 64
SHORT_CONV = 4
CONF_CONV = 31
NORM_EPS = 1e-6
N_BRANCH = 3
N_SMALL = 128
N_SMALL_T = 32
V7X_VMEM_LIMIT = 48 * 1024 * 1024


def _cparams(n_axes):
    return pltpu.CompilerParams(dimension_semantics=("arbitrary",) * n_axes,
                                vmem_limit_bytes=V7X_VMEM_LIMIT)


def _sigmoid(x):
    return 1.0 / (1.0 + jnp.exp(-x))


def _silu(x):
    return x * _sigmoid(x)


def _softplus(x):
    return jnp.maximum(x, 0.0) + jnp.log(1.0 + jnp.exp(-jnp.abs(x)))


def _dot(a, b):
    return jnp.dot(a, b, preferred_element_type=F32)


def _dot_nt(a, b):
    return lax.dot_general(a, b, (((1,), (1,)), ((), ())), preferred_element_type=F32)


def _dot_tn(a, b):
    return lax.dot_general(a, b, (((0,), (0,)), ((), ())), preferred_element_type=F32)


def _dot_f32(a, b):
    return jnp.dot(a, b, preferred_element_type=F32, precision=lax.Precision.HIGHEST)


def _prenorm_kernel(x_ref, w_ref, h_ref):
    x = x_ref[...]
    y = x * lax.rsqrt(jnp.mean(x * x, axis=-1, keepdims=True) + NORM_EPS)
    h_ref[...] = (y * w_ref[...]).astype(BF16)


def _prenorm(x2, w, tm):
    m, d = x2.shape
    return pl.pallas_call(
        _prenorm_kernel,
        out_shape=jax.ShapeDtypeStruct((m, d), BF16),
        grid=(m // tm,),
        in_specs=[pl.BlockSpec((tm, d), lambda i: (i, 0)),
                  pl.BlockSpec((1, d), lambda i: (0, 0))],
        out_specs=pl.BlockSpec((tm, d), lambda i: (i, 0)),
        compiler_params=_cparams(1),
        name="prenorm",
    )(x2, w.reshape(1, d))


def _proj_kernel(h_ref, w_ref, o_ref, *, epilogue):
    acc = _dot_nt(h_ref[...], w_ref[0])
    if epilogue == "f32":
        o_ref[...] = acc
    elif epilogue == "silu_bf16":
        o_ref[...] = _silu(acc).astype(BF16)
    elif epilogue == "bf16":
        o_ref[...] = acc.astype(BF16)
    else:
        raise ValueError(epilogue)


def _proj(h, w_packed, layer, seg_cols, seg_width, *, epilogue, tm, tn, out_dtype):
    m, k = h.shape
    n = len(seg_cols) * seg_width
    per_seg = seg_width // tn

    def w_tile(j):
        tile = seg_cols[0] // tn + j
        for s in range(1, len(seg_cols)):
            tile = jnp.where(j >= s * per_seg, seg_cols[s] // tn + j - s * per_seg, tile)
        return tile
    return pl.pallas_call(
        functools.partial(_proj_kernel, epilogue=epilogue),
        out_shape=jax.ShapeDtypeStruct((m, n), out_dtype),
        grid=(n // tn, m // tm),
        in_specs=[pl.BlockSpec((tm, k), lambda j, i: (i, 0)),
                  pl.BlockSpec((1, tn, k), lambda j, i: (layer, w_tile(j), 0))],
        out_specs=pl.BlockSpec((tm, tn), lambda j, i: (i, j)),
        compiler_params=_cparams(2),
        name="proj_" + epilogue,
    )(h, w_packed)


def _proj_t_kernel(wt_ref, h_ref, o_ref, *, scale):
    o_ref[0] = (_dot_nt(wt_ref[0], h_ref[...]) * scale).astype(BF16)


def _proj_t(h, w_packed, layer, row0, n, *, bsz, tm, tn, scale=1.0):
    m, k = h.shape
    seq = m // bsz
    nts = seq // tm
    return pl.pallas_call(
        functools.partial(_proj_t_kernel, scale=scale),
        out_shape=jax.ShapeDtypeStruct((bsz, n, seq), BF16),
        grid=(n // tn, m // tm),
        in_specs=[pl.BlockSpec((1, tn, k), lambda j, i: (layer, row0 // tn + j, 0)),
                  pl.BlockSpec((tm, k), lambda j, i: (i, 0))],
        out_specs=pl.BlockSpec((1, tn, tm), lambda j, i: (i // nts, j, i % nts)),
        compiler_params=_cparams(2),
        name="proj_t",
    )(w_packed, h)


def _glu_kernel(h_ref, wv_ref, wg_ref, o_ref):
    h = h_ref[...]
    o_ref[...] = _dot_nt(h, wv_ref[0]) * _sigmoid(_dot_nt(h, wg_ref[0]))


def _proj_glu(h, w_packed, layer, col0, n, *, tm, tn):
    m, k = h.shape
    return pl.pallas_call(
        _glu_kernel,
        out_shape=jax.ShapeDtypeStruct((m, n), F32),
        grid=(n // tn, m // tm),
        in_specs=[pl.BlockSpec((tm, k), lambda j, i: (i, 0)),
                  pl.BlockSpec((1, tn, k), lambda j, i: (layer, col0 // tn + j, 0)),
                  pl.BlockSpec((1, tn, k), lambda j, i: (layer, (col0 + n) // tn + j, 0))],
        out_specs=pl.BlockSpec((tm, tn), lambda j, i: (i, j)),
        compiler_params=_cparams(2),
        name="proj_glu",
    )(h, w_packed, w_packed)


def _gate_transform(x, exp_a, dt_bias, f_bias, idx):
    beta = _sigmoid(x)
    log_a = -exp_a * _softplus(x + dt_bias)
    log_f = -_softplus(-(x + f_bias))
    return jnp.where(idx < 8, beta, jnp.where(idx < 16, log_a, log_f))


def _small_kernel(h_ref, w_ref, pc_ref, o_ref, oc_ref):
    col = _dot_nt(h_ref[...], w_ref[0].astype(BF16))
    pc = pc_ref[0]
    cidx = lax.broadcasted_iota(jnp.int32, col.shape, 1)
    col = _gate_transform(col, pc[0:1, :], pc[1:2, :], pc[2:3, :], cidx)
    o_ref[...] = col
    row = col.T
    for c in range(row.shape[1] // CHUNK):
        oc_ref[c] = row[:N_SMALL_T, c * CHUNK:(c + 1) * CHUNK]


def _small_proj(h, w_small, prm_c, layer, tm):
    m, k = h.shape
    return pl.pallas_call(
        _small_kernel,
        out_shape=(jax.ShapeDtypeStruct((m, N_SMALL), F32),
                   jax.ShapeDtypeStruct((m // CHUNK, N_SMALL_T, CHUNK), F32)),
        grid=(m // tm,),
        in_specs=[pl.BlockSpec((tm, k), lambda i: (i, 0)),
                  pl.BlockSpec((1, N_SMALL, k), lambda i: (layer, 0, 0)),
                  pl.BlockSpec((1, 8, N_SMALL), lambda i: (layer, 0, 0))],
        out_specs=(pl.BlockSpec((tm, N_SMALL), lambda i: (i, 0)),
                   pl.BlockSpec((tm // CHUNK, N_SMALL_T, CHUNK), lambda i: (i, 0, 0))),
        compiler_params=_cparams(1),
        name="small_proj",
    )(h, w_small, prm_c)


def _split3(x):
    a = x.astype(BF16)
    r1 = x - a.astype(F32)
    b = r1.astype(BF16)
    c = (r1 - b.astype(F32)).astype(BF16)
    return a, b, c


LOG2_E = 1.4426950408889634
N_SPLIT = 3


def _cumsum_kernel(x_ref, o_ref, carry_ref):
    ts = x_ref.shape[1]
    nh = o_ref.shape[1]
    blk = 128

    @pl.when(pl.program_id(1) == 0)
    def _():
        carry_ref[...] = jnp.zeros_like(carry_ref)

    r = lax.broadcasted_iota(jnp.int32, (blk, blk), 0)
    c = lax.broadcasted_iota(jnp.int32, (blk, blk), 1)
    tril = jnp.where(r >= c, 1.0, 0.0).astype(BF16)
    er = lax.broadcasted_iota(jnp.int32, (blk, nh * blk), 0)
    ec = lax.broadcasted_iota(jnp.int32, (blk, nh * blk), 1)
    place = [jnp.where((er == 2 * nh + ec // blk) & (ec % blk == t), 1.0, 0.0).astype(BF16)
             for t in range(N_SPLIT)]

    carry = carry_ref[0:1, :]
    for j in range(ts // blk):
        rows = slice(j * blk, (j + 1) * blk)
        x1, x2, x3 = _split3(x_ref[0, rows, :])
        run = _dot(tril, x1) + _dot(tril, x2) + _dot(tril, x3) + carry
        carry = run[blk - 1:blk, :]
        parts = _split3(run * LOG2_E)
        rep = _dot(parts[0], place[0]) + _dot(parts[1], place[1]) + _dot(parts[2], place[2])
        for h in range(nh):
            o_ref[0, h, rows, :] = rep[:, h * blk:(h + 1) * blk].astype(BF16)
    carry_ref[0:1, :] = carry


def _forget_cumsum(small_c3, nh, ts):
    bsz, seq, _ = small_c3.shape
    return pl.pallas_call(
        _cumsum_kernel,
        out_shape=jax.ShapeDtypeStruct((bsz, nh, seq, 128), BF16),
        grid=(bsz, seq // ts),
        in_specs=[pl.BlockSpec((1, ts, N_SMALL), lambda b, i: (b, i, 0))],
        out_specs=pl.BlockSpec((1, nh, ts, 128), lambda b, i: (b, 0, i, 0)),
        scratch_shapes=[pltpu.VMEM((8, N_SMALL), F32)],
        compiler_params=_cparams(2),
        name="forget_cumsum",
    )(small_c3)


GDN_CHUNKS_PER_ITER = 2


def _gdn_local_kernel(qkv_ref, halo_ref, cw_ref, sc_ref, sr_ref,
                      u0_ref, w_ref, qd_ref, kt_ref, qk_ref, dec_ref, xx_ref, act_ref):
    i = pl.program_id(1)
    ts = qkv_ref.shape[1]
    bw = u0_ref.shape[2]
    nh = bw // HEAD_DIM
    halo = halo_ref[0]
    xx_ref[0:8, :] = jnp.where(i > 0, halo, 0.0)
    xx_ref[8:, :] = qkv_ref[0]

    cw = cw_ref[...]

    def conv_chunk(ci):
        x = xx_ref[ci * CHUNK:ci * CHUNK + CHUNK + 8, :]
        acc = cw[SHORT_CONV - 1:SHORT_CONV, :] * x[8:, :]
        for s in range(1, SHORT_CONV):
            k = SHORT_CONV - 1 - s
            acc = acc + cw[k:k + 1, :] * pltpu.roll(x, s, 0)[8:, :]
        act_ref[ci * CHUNK:(ci + 1) * CHUNK, :] = _silu(acc)

    r = lax.broadcasted_iota(jnp.int32, (CHUNK, CHUNK), 0)
    c = lax.broadcasted_iota(jnp.int32, (CHUNK, CHUNK), 1)
    incl = r >= c
    strict = r > c
    tril = incl.astype(F32)
    triu = (r <= c).astype(F32)
    eye = jnp.where(r == c, 1.0, 0.0).astype(F32)
    nchunks = ts // CHUNK
    cpi = GDN_CHUNKS_PER_ITER if nchunks % GDN_CHUNKS_PER_ITER == 0 else 1

    def iter_body(it, _):
        units = []
        for cc in range(cpi):
            ci = it * cpi + cc
            conv_chunk(ci)
            rows = pl.ds(ci * CHUNK, CHUNK)
            sc = sc_ref[0, rows, :]
            g_col = _dot_f32(tril, sc)
            g_row = _dot_f32(sr_ref[ci], triu)
            for h in range(nh):
                q = act_ref[rows, h * HEAD_DIM:(h + 1) * HEAD_DIM]
                k_ = act_ref[rows, bw + h * HEAD_DIM: bw + (h + 1) * HEAD_DIM]
                v = act_ref[rows, 2 * bw + h * HEAD_DIM: 2 * bw + (h + 1) * HEAD_DIM]
                q = q * (lax.rsqrt(jnp.sum(q * q, axis=-1, keepdims=True) + NORM_EPS) * HEAD_DIM ** -0.5)
                k_ = k_ * lax.rsqrt(jnp.sum(k_ * k_, axis=-1, keepdims=True) + NORM_EPS)
                beta = sc[:, h:h + 1]
                g = g_col[:, nh + h:nh + h + 1]
                g_last = g_col[CHUNK - 1:CHUNK, nh + h:nh + h + 1]
                gr = g_row[nh + h:nh + h + 1, :]
                units.append(dict(
                    ci=ci, rows=rows, h=h, q=q, k=k_, v=v, beta=beta, g=g, g_last=g_last,
                    decay=jnp.exp(jnp.where(incl, g - gr, -jnp.inf)),
                    eg=jnp.exp(g), kb=k_.astype(BF16), qb=q.astype(BF16)))

        for u in units:
            u["kq"] = _dot_nt(jnp.concatenate([u["kb"], u["qb"]], axis=0), u["kb"])
        for u in units:
            u["qk"] = u["kq"][CHUNK:] * u["decay"]
            a = jnp.where(strict, -(u["beta"] * u["kq"][:CHUNK] * u["decay"]), 0.0)
            u["x"] = eye + a
            u["p"] = a
        for u in units:
            pb = u["p"].astype(BF16)
            u["p"] = _dot(pb, pb)
        m = 2
        while 2 * m < CHUNK:
            for u in units:
                pb = u["p"].astype(BF16)
                out = _dot(jnp.concatenate([u["x"].astype(BF16), pb], axis=0), pb)
                u["x"] = u["x"] + out[:CHUNK]
                u["p"] = out[CHUNK:]
            m *= 2
        for u in units:
            u["x"] = u["x"] + _dot(u["x"].astype(BF16), u["p"].astype(BF16))
        for u in units:
            rhs = jnp.concatenate([u["v"] * u["beta"], u["k"] * (u["beta"] * u["eg"])], axis=1)
            u["uw"] = _dot(u["x"].astype(BF16), rhs.astype(BF16))

        for u in units:
            h, rows = u["h"], u["rows"]
            sl = slice(h * HEAD_DIM, (h + 1) * HEAD_DIM)
            u0_ref[0, rows, sl] = u["uw"][:, :HEAD_DIM]
            w_ref[0, rows, sl] = u["uw"][:, HEAD_DIM:].astype(BF16)
            qd_ref[0, rows, sl] = (u["q"] * u["eg"]).astype(BF16)
            kt_ref[0, rows, sl] = (u["k"] * jnp.exp(u["g_last"] - u["g"])).astype(BF16)
            qk_ref[0, rows, h * CHUNK:(h + 1) * CHUNK] = u["qk"].astype(BF16)
            dec_ref[0, u["ci"], :, sl] = jnp.broadcast_to(jnp.exp(u["g_last"]), (1, HEAD_DIM))
        return 0

    for it in range(nchunks // cpi):
        iter_body(it, 0)


def _gdn_local(qkv_a, conv_w, small_c, small_ch, ts):
    bsz, seq, n3 = qkv_a.shape
    bw = n3 // 3
    nh = bw // HEAD_DIM
    nt = seq // ts
    nc = ts // CHUNK
    big = lambda b, i: (b, i, 0)
    return pl.pallas_call(
        _gdn_local_kernel,
        out_shape=(jax.ShapeDtypeStruct((bsz, seq, bw), F32),
                   jax.ShapeDtypeStruct((bsz, seq, bw), BF16),
                   jax.ShapeDtypeStruct((bsz, seq, bw), BF16),
                   jax.ShapeDtypeStruct((bsz, seq, bw), BF16),
                   jax.ShapeDtypeStruct((bsz, seq, nh * CHUNK), BF16),
                   jax.ShapeDtypeStruct((bsz, seq // CHUNK, 1, bw), F32)),
        grid=(bsz, nt),
        in_specs=[pl.BlockSpec((1, ts, n3), big),
                  pl.BlockSpec((1, 8, n3), lambda b, i: (b, jnp.maximum(i * (ts // 8) - 1, 0), 0)),
                  pl.BlockSpec((SHORT_CONV, n3), lambda b, i: (0, 0)),
                  pl.BlockSpec((1, ts, N_SMALL), big),
                  pl.BlockSpec((nc, N_SMALL_T, CHUNK), lambda b, i: (b * nt + i, 0, 0))],
        out_specs=(pl.BlockSpec((1, ts, bw), big),
                   pl.BlockSpec((1, ts, bw), big),
                   pl.BlockSpec((1, ts, bw), big),
                   pl.BlockSpec((1, ts, bw), big),
                   pl.BlockSpec((1, ts, nh * CHUNK), big),
                   pl.BlockSpec((1, nc, 1, bw), lambda b, i: (b, i, 0, 0))),
        scratch_shapes=[pltpu.VMEM((ts + 8, n3), F32), pltpu.VMEM((ts, n3), F32)],
        compiler_params=_cparams(2),
        name="gdn_local",
    )(qkv_a, qkv_a, conv_w, small_c.reshape(bsz, seq, N_SMALL), small_ch)


def _gdn_scan_kernel(u0_ref, w_ref, qd_ref, kt_ref, qk_ref, dec_ref, z_ref, nw_ref, y_ref, state_ref):
    i = pl.program_id(0)
    bsz, ts, bw = u0_ref.shape
    nh = bw // HEAD_DIM

    @pl.when(i == 0)
    def _():
        state_ref[...] = jnp.zeros_like(state_ref)

    nw = nw_ref[...]

    def chunk_body(ci, _):
        rows = pl.ds(pl.multiple_of(ci * CHUNK, CHUNK), CHUNK)
        units = []
        for b in range(bsz):
            for h in range(nh):
                sl = slice(h * HEAD_DIM, (h + 1) * HEAD_DIM)
                state = state_ref[b * nh + h]
                lhs = jnp.concatenate([w_ref[b, rows, sl], qd_ref[b, rows, sl]], axis=0)
                units.append(dict(b=b, h=h, sl=sl, state=state, ws=_dot(lhs, state.astype(BF16))))
        for u in units:
            b, h, sl = u["b"], u["h"], u["sl"]
            ub = (u0_ref[b, rows, sl] - u["ws"][:CHUNK]).astype(BF16)
            u["o"] = u["ws"][CHUNK:] + _dot(qk_ref[b, rows, h * CHUNK:(h + 1) * CHUNK], ub)
            u["ds"] = _dot_tn(kt_ref[b, rows, sl], ub)
        for u in units:
            b, h, sl = u["b"], u["h"], u["sl"]
            state_ref[b * nh + h] = u["state"] * dec_ref[b, ci, :, sl] + u["ds"]
            o = u["o"]
            o = o * lax.rsqrt(jnp.mean(o * o, axis=-1, keepdims=True) + NORM_EPS)
            y_ref[b, rows, sl] = (o * nw * z_ref[b, rows, sl].astype(F32)).astype(BF16)
        return 0

    lax.fori_loop(0, ts // CHUNK, chunk_body, 0)


def _gdn_scan(u0, w, qd, kt, qk, dec, z_all, o_norm_w, ts):
    bsz, seq, bw = u0.shape
    nh = bw // HEAD_DIM
    big = lambda i: (0, i, 0)
    return pl.pallas_call(
        _gdn_scan_kernel,
        out_shape=jax.ShapeDtypeStruct((bsz, seq, bw), BF16),
        grid=(seq // ts,),
        in_specs=[pl.BlockSpec((bsz, ts, bw), big),
                  pl.BlockSpec((bsz, ts, bw), big),
                  pl.BlockSpec((bsz, ts, bw), big),
                  pl.BlockSpec((bsz, ts, bw), big),
                  pl.BlockSpec((bsz, ts, nh * CHUNK), big),
                  pl.BlockSpec((bsz, ts // CHUNK, 1, bw), lambda i: (0, i, 0, 0)),
                  pl.BlockSpec((bsz, ts, bw), big),
                  pl.BlockSpec((1, HEAD_DIM), lambda i: (0, 0))],
        out_specs=pl.BlockSpec((bsz, ts, bw), big),
        scratch_shapes=[pltpu.VMEM((bsz * nh, HEAD_DIM, HEAD_DIM), F32)],
        compiler_params=_cparams(1),
        name="gdn_scan",
    )(u0, w, qd, kt, qk, dec, z_all, o_norm_w.reshape(1, HEAD_DIM))


def _conformer_kernel(u_ref, halo_ref, z_ref, cw_ref, cb_ref, lw_ref, lb_ref, y_ref, xx_ref):
    i = pl.program_id(1)
    ts = u_ref.shape[1]
    pad = halo_ref.shape[1]
    xx_ref[0:pad, :] = jnp.where(i > 0, halo_ref[0], 0.0)
    xx_ref[pad:, :] = u_ref[0]
    cw = cw_ref[...]
    off = pad - (CONF_CONV - 1)
    acc = None
    for r in range(8):
        rows = ts if r == 0 else ts + 8
        y = None
        for m in range((pad + 8) // 8):
            k = 8 * m + r - off
            if 0 <= k < CONF_CONV:
                term = cw[k:k + 1, :] * xx_ref[8 * m:8 * m + rows, :]
                y = term if y is None else y + term
        shifted = y[r:r + ts, :]
        acc = shifted if acc is None else acc + shifted
    u = acc + cb_ref[...]
    xc = u - jnp.mean(u, axis=-1, keepdims=True)
    y = xc * lax.rsqrt(jnp.mean(xc * xc, axis=-1, keepdims=True) + NORM_EPS)
    y = _silu(y * lw_ref[...] + lb_ref[...])
    y_ref[0] = (y * z_ref[0].astype(F32)).astype(BF16)


def _conformer(u, z_all, conv_w, conv_b, ln_w, ln_b, ts):
    bsz, seq, bw = u.shape
    pad = 32
    big = lambda b, i: (b, i, 0)
    vec = lambda b, i: (0, 0)
    return pl.pallas_call(
        _conformer_kernel,
        out_shape=jax.ShapeDtypeStruct((bsz, seq, bw), BF16),
        grid=(bsz, seq // ts),
        in_specs=[pl.BlockSpec((1, ts, bw), big),
                  pl.BlockSpec((1, pad, bw), lambda b, i: (b, jnp.maximum(i * (ts // pad) - 1, 0), 0)),
                  pl.BlockSpec((1, ts, bw), lambda b, i: (b, i, 1)),
                  pl.BlockSpec((CONF_CONV, bw), vec),
                  pl.BlockSpec((1, bw), vec),
                  pl.BlockSpec((1, bw), vec),
                  pl.BlockSpec((1, bw), vec)],
        out_specs=pl.BlockSpec((1, ts, bw), big),
        scratch_shapes=[pltpu.VMEM((ts + pad, bw), F32)],
        compiler_params=_cparams(2),
        name="conformer",
    )(u, u, z_all, conv_w, conv_b.reshape(1, bw), ln_w.reshape(1, bw), ln_b.reshape(1, bw))


FOX_SUB = 256
FOX_HEADS = 4


def _fox_kernel(qi_ref, kj_ref, qt_ref, k_ref, vt_ref, c_ref, z_ref, y_ref, m_ref, l_ref, acc_ref):
    t = pl.program_id(2)
    i = qi_ref[t]
    j = kj_ref[t]
    tq = qt_ref.shape[2]
    tk = k_ref.shape[1]
    hp = c_ref.shape[1]
    nsub = tq // FOX_SUB
    streams = [(g, n) for g in range(hp) for n in range(nsub)]

    @pl.when(j == 0)
    def _():
        m_ref[...] = jnp.full_like(m_ref, -jnp.inf)
        l_ref[...] = jnp.zeros_like(l_ref)
        acc_ref[...] = jnp.zeros_like(acc_ref)

    gate_rows = lax.broadcasted_iota(jnp.int32, (c_ref.shape[3], FOX_SUB), 0)
    minus_gate = jnp.where(gate_rows < N_SPLIT, -1.0, 0.0).astype(BF16)

    def step(diag):
        rows = [min((n + 1) * FOX_SUB, tk) if diag else tk for n in range(nsub)]
        s_all = []
        for g, n in streams:
            hd = slice(g * HEAD_DIM, (g + 1) * HEAD_DIM)
            k_aug = jnp.concatenate([k_ref[0, :rows[n], hd], c_ref[0, g, :rows[n], :]], axis=1)
            q_aug = jnp.concatenate([qt_ref[0, hd, n * FOX_SUB:(n + 1) * FOX_SUB], minus_gate], axis=0)
            s_all.append(_dot(k_aug, q_aug))
        for (g, n), s in zip(streams, s_all):
            hd = slice(g * HEAD_DIM, (g + 1) * HEAD_DIM)
            cols = slice(n * FOX_SUB, (n + 1) * FOX_SUB)
            if diag:
                kpos = lax.broadcasted_iota(jnp.int32, s.shape, 0)
                qpos = lax.broadcasted_iota(jnp.int32, s.shape, 1) + n * FOX_SUB
                s = jnp.where(kpos <= qpos, s, -jnp.inf)
            m_prev = m_ref[g:g + 1, cols]
            m_new = jnp.maximum(m_prev, jnp.max(s, axis=0, keepdims=True))
            p = jnp.exp2(s - m_new)
            alpha = jnp.exp2(m_prev - m_new)
            l_ref[g:g + 1, cols] = alpha * l_ref[g:g + 1, cols] + jnp.sum(p, axis=0, keepdims=True)
            acc_ref[hd, cols] = alpha * acc_ref[hd, cols] + _dot(vt_ref[0, hd, :rows[n]], p.astype(BF16))
            m_ref[g:g + 1, cols] = m_new

    @pl.when(j < i)
    def _():
        step(False)

    @pl.when(j == i)
    def _():
        step(True)
        for g in range(hp):
            hd = slice(g * HEAD_DIM, (g + 1) * HEAD_DIM)
            o = (acc_ref[hd, :] / l_ref[g:g + 1, :]).T
            y_ref[0, :, hd] = (o * z_ref[0, :, hd].astype(F32)).astype(BF16)


def _fox_attention(q_t, k, v_t, c_split, z_all, t):
    bsz, bw, seq = q_t.shape
    nh = bw // HEAD_DIM
    hp = FOX_HEADS if nh % FOX_HEADS == 0 else 1
    hw = hp * HEAD_DIM
    nt = seq // t
    steps = [(i, j) for i in range(nt) for j in range(i + 1)]
    qi = jnp.asarray([s[0] for s in steps], jnp.int32)
    kj = jnp.asarray([s[1] for s in steps], jnp.int32)
    grid_spec = pltpu.PrefetchScalarGridSpec(
        num_scalar_prefetch=2,
        grid=(bsz, nh // hp, len(steps)),
        in_specs=[pl.BlockSpec((1, hw, t), lambda b, h, s, qi, kj: (b, h, qi[s])),
                  pl.BlockSpec((1, t, hw), lambda b, h, s, qi, kj: (b, kj[s], h)),
                  pl.BlockSpec((1, hw, t), lambda b, h, s, qi, kj: (b, h, kj[s])),
                  pl.BlockSpec((1, hp, t, 128), lambda b, h, s, qi, kj: (b, h, kj[s], 0)),
                  pl.BlockSpec((1, t, hw), lambda b, h, s, qi, kj: (b, qi[s], 2 * (nh // hp) + h))],
        out_specs=pl.BlockSpec((1, t, hw), lambda b, h, s, qi, kj: (b, qi[s], h)),
        scratch_shapes=[pltpu.VMEM((hp, t), F32), pltpu.VMEM((hp, t), F32), pltpu.VMEM((hw, t), F32)],
    )
    return pl.pallas_call(
        _fox_kernel,
        out_shape=jax.ShapeDtypeStruct((bsz, seq, bw), BF16),
        grid_spec=grid_spec,
        compiler_params=_cparams(3),
        name="fox_attention",
    )(qi, kj, q_t, k, v_t, c_split, z_all)


def _merge_kernel(h_ref, ya_ref, yb_ref, yc_ref, wg0_ref, wg1_ref, wg2_ref, wb0_ref, wb1_ref, wb2_ref, o_ref):
    h = h_ref[...]
    acc = None
    for y_ref, wg_ref, wb_ref in ((ya_ref, wg0_ref, wb0_ref), (yb_ref, wg1_ref, wb1_ref), (yc_ref, wg2_ref, wb2_ref)):
        term = _sigmoid(_dot_nt(h, wg_ref[0])) * _dot(y_ref[...], wb_ref[0, 0])
        acc = term if acc is None else acc + term
    o_ref[...] = acc.astype(BF16)


def _merge(h, ya, yb, yc, w_packed, gate_col0, w_branch, layer, tm, tn):
    m, d = h.shape
    bw = ya.shape[1]
    nj = d // tn
    g0 = gate_col0 // tn
    row = lambda j, i: (i, 0)
    return pl.pallas_call(
        _merge_kernel,
        out_shape=jax.ShapeDtypeStruct((m, d), BF16),
        grid=(nj, m // tm),
        in_specs=[pl.BlockSpec((tm, d), row),
                  pl.BlockSpec((tm, bw), row), pl.BlockSpec((tm, bw), row), pl.BlockSpec((tm, bw), row),
                  pl.BlockSpec((1, tn, d), lambda j, i: (layer, g0 + j, 0)),
                  pl.BlockSpec((1, tn, d), lambda j, i: (layer, g0 + nj + j, 0)),
                  pl.BlockSpec((1, tn, d), lambda j, i: (layer, g0 + 2 * nj + j, 0)),
                  pl.BlockSpec((1, 1, bw, tn), lambda j, i: (layer, 0, 0, j)),
                  pl.BlockSpec((1, 1, bw, tn), lambda j, i: (layer, 1, 0, j)),
                  pl.BlockSpec((1, 1, bw, tn), lambda j, i: (layer, 2, 0, j))],
        out_specs=pl.BlockSpec((tm, tn), lambda j, i: (i, j)),
        compiler_params=_cparams(2),
        name="gated_merge",
    )(h, ya, yb, yc, w_packed, w_packed, w_packed, w_branch, w_branch, w_branch)


def _out_kernel(m_ref, w_ref, x_ref, pw_ref, o_ref):
    out = _dot(m_ref[...], w_ref[0])
    y = out * lax.rsqrt(jnp.mean(out * out, axis=-1, keepdims=True) + NORM_EPS)
    o_ref[...] = x_ref[...] + y * pw_ref[...]


def _out_proj(merged, w_out, layer, x2, post_w, tm):
    m, d = x2.shape
    return pl.pallas_call(
        _out_kernel,
        out_shape=jax.ShapeDtypeStruct((m, d), F32),
        grid=(m // tm,),
        in_specs=[pl.BlockSpec((tm, d), lambda i: (i, 0)),
                  pl.BlockSpec((1, d, d), lambda i: (layer, 0, 0)),
                  pl.BlockSpec((tm, d), lambda i: (i, 0)),
                  pl.BlockSpec((1, d), lambda i: (0, 0))],
        out_specs=pl.BlockSpec((tm, d), lambda i: (i, 0)),
        compiler_params=_cparams(1),
        name="out_proj",
    )(merged, w_out, x2, post_w.reshape(1, d))


def _layer(x2, bsz, seq, l, shared, p):
    m, d = x2.shape
    bw = d // 2
    tm = min(1024, m)
    h = _prenorm(x2, p["pre_w"], min(512, m))

    wp = shared["w_packed"]
    col = _packed_columns(d)
    qkv_a = _proj(h, wp, l, (col["qkv_a"],), 3 * bw, epilogue="f32", tm=tm, tn=1024, out_dtype=F32)
    z_all = _proj(h, wp, l, (col["z_a"], col["z_b"], col["z_c"]), bw, epilogue="silu_bf16", tm=tm, tn=1024,
                  out_dtype=BF16)
    u_b = _proj_glu(h, wp, l, col["glu"], bw, tm=tm, tn=512)
    q_t = _proj_t(h, wp, l, col["q_c"], bw, bsz=bsz, tm=min(1024, seq), tn=1024,
                  scale=HEAD_DIM ** -0.5 * LOG2_E)
    k_c = _proj(h, wp, l, (col["k_c"],), bw, epilogue="bf16", tm=tm, tn=1024, out_dtype=BF16)
    v_t = _proj_t(h, wp, l, col["v_c"], bw, bsz=bsz, tm=min(1024, seq), tn=1024)
    small_c, small_ch = _small_proj(h, shared["w_small"], shared["prm_c"], l, min(512, seq))

    z3 = z_all.reshape(bsz, seq, 3 * bw)
    u0, w, qd, kt, qk, dec = _gdn_local(qkv_a.reshape(bsz, seq, 3 * bw), p["conv_qkv_w"], small_c, small_ch,
                                        min(256, seq))
    y_a = _gdn_scan(u0, w, qd, kt, qk, dec, z3, p["o_norm_w"], min(512, seq))

    y_b = _conformer(u_b.reshape(bsz, seq, bw), z3, p["conv_w"], p["conv_b"], p["ln_w"], p["ln_b"], min(256, seq))

    c_split = _forget_cumsum(small_c.reshape(bsz, seq, N_SMALL), bw // HEAD_DIM, min(1024, seq))
    y_c = _fox_attention(q_t, k_c.reshape(bsz, seq, bw), v_t, c_split, z3, min(1024, seq))

    merged = _merge(h, y_a.reshape(m, bw), y_b.reshape(m, bw), y_c.reshape(m, bw),
                    wp, col["gate"], shared["w_branch"], l, min(512, m), 512)
    return _out_proj(merged, shared["w_out"], l, x2, p["post_w"], min(256, m))


REPACK_TR = 512


def _packed_columns(d):
    bw = d // 2
    names = (("qkv_a", 3 * bw), ("z_a", bw), ("glu", 2 * bw), ("z_b", bw), ("q_c", bw), ("k_c", bw),
             ("v_c", bw), ("z_c", bw), ("gate", N_BRANCH * d))
    col, off = {}, 0
    for name, width in names:
        col[name] = off
        off += width
    col["total"] = off
    return col


def _repack_kernel(x_ref, o_ref):
    o_ref[...] = x_ref[...].astype(BF16)


def _repack_w_in(w_t, nh):
    depth, n_in, d = w_t.shape
    col = _packed_columns(d)
    tr = REPACK_TR
    t_glu, t_gate = col["glu"] // tr, col["gate"] // tr
    assert col["glu"] % tr == 0 and col["gate"] % tr == 0 and col["total"] % tr == 0
    assert col["total"] + 3 * nh == n_in

    def src_row(t):
        row = t * tr + jnp.where(t >= t_gate, 3 * nh, jnp.where(t >= t_glu, 2 * nh, 0))
        return pl.multiple_of(row, nh)
    return pl.pallas_call(
        _repack_kernel,
        out_shape=jax.ShapeDtypeStruct((depth, col["total"], d), BF16),
        grid=(depth, col["total"] // tr),
        in_specs=[pl.BlockSpec((pl.Element(1), pl.Element(tr), pl.Element(d)), lambda l, t: (l, src_row(t), 0))],
        out_specs=pl.BlockSpec((1, tr, d), lambda l, t: (l, t, 0)),
        compiler_params=_cparams(2),
        name="repack_w_in",
    )(w_t)


def _small_w_kernel(x_ref, y_ref, o_ref):
    o_ref[...] = jnp.zeros_like(o_ref)
    nx, ny = x_ref.shape[1], y_ref.shape[1]
    o_ref[0, 0:nx, :] = x_ref[0]
    o_ref[0, nx:nx + ny, :] = y_ref[0]


def _small_w(w_t, nh):
    depth, _, d = w_t.shape
    bw = d // 2
    beta0 = 4 * bw
    forget0 = beta0 + 2 * nh + 7 * bw
    return pl.pallas_call(
        _small_w_kernel,
        out_shape=jax.ShapeDtypeStruct((depth, N_SMALL, d), F32),
        grid=(depth,),
        in_specs=[pl.BlockSpec((pl.Element(1), pl.Element(2 * nh), pl.Element(d)), lambda l: (l, beta0, 0)),
                  pl.BlockSpec((pl.Element(1), pl.Element(nh), pl.Element(d)), lambda l: (l, forget0, 0))],
        out_specs=pl.BlockSpec((1, N_SMALL, d), lambda l: (l, 0, 0)),
        compiler_params=_cparams(1),
        name="small_w",
    )(w_t, w_t)


def _prep_params(pre_norm_w, post_norm_w, w_in, conv_qkv_w, a_log, dt_bias, o_norm_w,
                 conv_w, conv_b, ln_w, ln_b, f_bias, w_branch, w_out):
    depth, d, _ = w_in.shape
    bw = d // 2
    nh = bw // HEAD_DIM
    w_t = jnp.swapaxes(w_in, 1, 2)
    w_packed = _repack_w_in(w_t, nh)
    w_small = _small_w(w_t, nh)
    zeros = jnp.zeros((depth, nh), F32)
    ones = jnp.ones((depth, nh), F32)
    vec24 = lambda a, b, c: jnp.concatenate([a, b, c], axis=1)
    prm = jnp.stack([vec24(ones, jnp.exp(a_log.astype(F32)), ones),
                     vec24(zeros, dt_bias.astype(F32), zeros),
                     vec24(zeros, zeros, f_bias.astype(F32))], axis=1)
    prm_c = jnp.pad(prm, ((0, 0), (0, 8 - 3), (0, N_SMALL - 3 * nh)))
    shared = {"w_packed": w_packed, "w_small": w_small, "prm_c": prm_c,
              "w_branch": w_branch.astype(BF16), "w_out": w_out.astype(BF16)}
    per_layer = {"pre_w": pre_norm_w, "post_w": post_norm_w, "conv_qkv_w": conv_qkv_w, "o_norm_w": o_norm_w,
                 "conv_w": conv_w, "conv_b": conv_b, "ln_w": ln_w, "ln_b": ln_b}
    return shared, per_layer


def kernel(x, pre_norm_w, post_norm_w, w_in, conv_qkv_w, a_log, dt_bias, o_norm_w, conv_w, conv_b, ln_w, ln_b, f_bias, w_branch, w_out):
    bsz, seq, d = x.shape
    depth = w_in.shape[0]
    shared, per_layer = _prep_params(pre_norm_w, post_norm_w, w_in, conv_qkv_w, a_log, dt_bias, o_norm_w,
                                     conv_w, conv_b, ln_w, ln_b, f_bias, w_branch, w_out)
    x2 = x.reshape(bsz * seq, d)
    for l in range(depth):
        x2 = _layer(x2, bsz, seq, l, shared, {k: v[l] for k, v in per_layer.items()})
    return x2.reshape(bsz, seq, d)
```

```python
import functools

import jax
import jax.numpy as jnp
from jax import lax
from jax.experimental import pallas as pl
from jax.experimental.pallas import tpu as pltpu

F32 = jnp.float32
BF16 = jnp.bfloat16

HEAD_DIM = 128
CHUNK = 64
SHORT_CONV = 4
CONF_CONV = 31
NORM_EPS = 1e-6
N_BRANCH = 3
N_SMALL = 128
N_SMALL_T = 32
V7X_VMEM_LIMIT = 48 * 1024 * 1024


def _cparams(n_axes):
    return pltpu.CompilerParams(dimension_semantics=("arbitrary",) * n_axes,
                                vmem_limit_bytes=V7X_VMEM_LIMIT)


def _sigmoid(x):
    return 1.0 / (1.0 + jnp.exp(-x))


def _silu(x):
    return x * _sigmoid(x)


def _softplus(x):
    return jnp.maximum(x, 0.0) + jnp.log(1.0 + jnp.exp(-jnp.abs(x)))


def _dot(a, b):
    return jnp.dot(a, b, preferred_element_type=F32)


def _dot_nt(a, b):
    return lax.dot_general(a, b, (((1,), (1,)), ((), ())), preferred_element_type=F32)


def _dot_tn(a, b):
    return lax.dot_general(a, b, (((0,), (0,)), ((), ())), preferred_element_type=F32)


def _dot_f32(a, b):
    return jnp.dot(a, b, preferred_element_type=F32, precision=lax.Precision.HIGHEST)


def _prenorm_kernel(x_ref, w_ref, h_ref):
    x = x_ref[...]
    y = x * lax.rsqrt(jnp.mean(x * x, axis=-1, keepdims=True) + NORM_EPS)
    h_ref[...] = (y * w_ref[...]).astype(BF16)


def _prenorm(x2, w, tm):
    m, d = x2.shape
    return pl.pallas_call(
        _prenorm_kernel,
        out_shape=jax.ShapeDtypeStruct((m, d), BF16),
        grid=(m // tm,),
        in_specs=[pl.BlockSpec((tm, d), lambda i: (i, 0)),
                  pl.BlockSpec((1, d), lambda i: (0, 0))],
        out_specs=pl.BlockSpec((tm, d), lambda i: (i, 0)),
        compiler_params=_cparams(1),
        name="prenorm",
    )(x2, w.reshape(1, d))


def _proj_kernel(h_ref, w_ref, o_ref, *, epilogue):
    acc = _dot_nt(h_ref[...], w_ref[0])
    if epilogue == "f32":
        o_ref[...] = acc
    elif epilogue == "silu_bf16":
        o_ref[...] = _silu(acc).astype(BF16)
    elif epilogue == "bf16":
        o_ref[...] = acc.astype(BF16)
    else:
        raise ValueError(epilogue)


def _proj(h, w_packed, layer, seg_cols, seg_width, *, epilogue, tm, tn, out_dtype):
    m, k = h.shape
    n = len(seg_cols) * seg_width
    per_seg = seg_width // tn

    def w_tile(j):
        tile = seg_cols[0] // tn + j
        for s in range(1, len(seg_cols)):
            tile = jnp.where(j >= s * per_seg, seg_cols[s] // tn + j - s * per_seg, tile)
        return tile
    return pl.pallas_call(
        functools.partial(_proj_kernel, epilogue=epilogue),
        out_shape=jax.ShapeDtypeStruct((m, n), out_dtype),
        grid=(n // tn, m // tm),
        in_specs=[pl.BlockSpec((tm, k), lambda j, i: (i, 0)),
                  pl.BlockSpec((1, tn, k), lambda j, i: (layer, w_tile(j), 0))],
        out_specs=pl.BlockSpec((tm, tn), lambda j, i: (i, j)),
        compiler_params=_cparams(2),
        name="proj_" + epilogue,
    )(h, w_packed)


def _proj_t_kernel(wt_ref, h_ref, o_ref, *, scale):
    o_ref[0] = (_dot_nt(wt_ref[0], h_ref[...]) * scale).astype(BF16)


def _proj_t(h, w_packed, layer, row0, n, *, bsz, tm, tn, scale=1.0):
    m, k = h.shape
    seq = m // bsz
    nts = seq // tm
    return pl.pallas_call(
        functools.partial(_proj_t_kernel, scale=scale),
        out_shape=jax.ShapeDtypeStruct((bsz, n, seq), BF16),
        grid=(n // tn, m // tm),
        in_specs=[pl.BlockSpec((1, tn, k), lambda j, i: (layer, row0 // tn + j, 0)),
                  pl.BlockSpec((tm, k), lambda j, i: (i, 0))],
        out_specs=pl.BlockSpec((1, tn, tm), lambda j, i: (i // nts, j, i % nts)),
        compiler_params=_cparams(2),
        name="proj_t",
    )(w_packed, h)


def _glu_kernel(h_ref, wv_ref, wg_ref, o_ref):
    h = h_ref[...]
    o_ref[...] = _dot_nt(h, wv_ref[0]) * _sigmoid(_dot_nt(h, wg_ref[0]))


def _proj_glu(h, w_packed, layer, col0, n, *, tm, tn):
    m, k = h.shape
    return pl.pallas_call(
        _glu_kernel,
        out_shape=jax.ShapeDtypeStruct((m, n), F32),
        grid=(n // tn, m // tm),
        in_specs=[pl.BlockSpec((tm, k), lambda j, i: (i, 0)),
                  pl.BlockSpec((1, tn, k), lambda j, i: (layer, col0 // tn + j, 0)),
                  pl.BlockSpec((1, tn, k), lambda j, i: (layer, (col0 + n) // tn + j, 0))],
        out_specs=pl.BlockSpec((tm, tn), lambda j, i: (i, j)),
        compiler_params=_cparams(2),
        name="proj_glu",
    )(h, w_packed, w_packed)


def _gate_transform(x, exp_a, dt_bias, f_bias, idx):
    beta = _sigmoid(x)
    log_a = -exp_a * _softplus(x + dt_bias)
    log_f = -_softplus(-(x + f_bias))
    return jnp.where(idx < 8, beta, jnp.where(idx < 16, log_a, log_f))


def _small_kernel(h_ref, w_ref, pc_ref, o_ref, oc_ref):
    col = _dot_nt(h_ref[...], w_ref[0].astype(BF16))
    pc = pc_ref[0]
    cidx = lax.broadcasted_iota(jnp.int32, col.shape, 1)
    col = _gate_transform(col, pc[0:1, :], pc[1:2, :], pc[2:3, :], cidx)
    o_ref[...] = col
    row = col.T
    for c in range(row.shape[1] // CHUNK):
        oc_ref[c] = row[:N_SMALL_T, c * CHUNK:(c + 1) * CHUNK]


def _small_proj(h, w_small, prm_c, layer, tm):
    m, k = h.shape
    return pl.pallas_call(
        _small_kernel,
        out_shape=(jax.ShapeDtypeStruct((m, N_SMALL), F32),
                   jax.ShapeDtypeStruct((m // CHUNK, N_SMALL_T, CHUNK), F32)),
        grid=(m // tm,),
        in_specs=[pl.BlockSpec((tm, k), lambda i: (i, 0)),
                  pl.BlockSpec((1, N_SMALL, k), lambda i: (layer, 0, 0)),
                  pl.BlockSpec((1, 8, N_SMALL), lambda i: (layer, 0, 0))],
        out_specs=(pl.BlockSpec((tm, N_SMALL), lambda i: (i, 0)),
                   pl.BlockSpec((tm // CHUNK, N_SMALL_T, CHUNK), lambda i: (i, 0, 0))),
        compiler_params=_cparams(1),
        name="small_proj",
    )(h, w_small, prm_c)


def _split3(x):
    a = x.astype(BF16)
    r1 = x - a.astype(F32)
    b = r1.astype(BF16)
    c = (r1 - b.astype(F32)).astype(BF16)
    return a, b, c


LOG2_E = 1.4426950408889634
N_SPLIT = 3


def _cumsum_kernel(x_ref, o_ref, carry_ref):
    ts = x_ref.shape[1]
    nh = o_ref.shape[1]
    blk = 128

    @pl.when(pl.program_id(1) == 0)
    def _():
        carry_ref[...] = jnp.zeros_like(carry_ref)

    r = lax.broadcasted_iota(jnp.int32, (blk, blk), 0)
    c = lax.broadcasted_iota(jnp.int32, (blk, blk), 1)
    tril = jnp.where(r >= c, 1.0, 0.0).astype(BF16)
    er = lax.broadcasted_iota(jnp.int32, (blk, nh * blk), 0)
    ec = lax.broadcasted_iota(jnp.int32, (blk, nh * blk), 1)
    place = [jnp.where((er == 2 * nh + ec // blk) & (ec % blk == t), 1.0, 0.0).astype(BF16)
             for t in range(N_SPLIT)]

    carry = carry_ref[0:1, :]
    for j in range(ts // blk):
        rows = slice(j * blk, (j + 1) * blk)
        x1, x2, x3 = _split3(x_ref[0, rows, :])
        run = _dot(tril, x1) + _dot(tril, x2) + _dot(tril, x3) + carry
        carry = run[blk - 1:blk, :]
        parts = _split3(run * LOG2_E)
        rep = _dot(parts[0], place[0]) + _dot(parts[1], place[1]) + _dot(parts[2], place[2])
        for h in range(nh):
            o_ref[0, h, rows, :] = rep[:, h * blk:(h + 1) * blk].astype(BF16)
    carry_ref[0:1, :] = carry


def _forget_cumsum(small_c3, nh, ts):
    bsz, seq, _ = small_c3.shape
    return pl.pallas_call(
        _cumsum_kernel,
        out_shape=jax.ShapeDtypeStruct((bsz, nh, seq, 128), BF16),
        grid=(bsz, seq // ts),
        in_specs=[pl.BlockSpec((1, ts, N_SMALL), lambda b, i: (b, i, 0))],
        out_specs=pl.BlockSpec((1, nh, ts, 128), lambda b, i: (b, 0, i, 0)),
        scratch_shapes=[pltpu.VMEM((8, N_SMALL), F32)],
        compiler_params=_cparams(2),
        name="forget_cumsum",
    )(small_c3)


GDN_CHUNKS_PER_ITER = 2


def _gdn_local_kernel(qkv_ref, halo_ref, cw_ref, sc_ref, sr_ref,
                      u0_ref, w_ref, qd_ref, kt_ref, qk_ref, dec_ref, xx_ref, act_ref):
    i = pl.program_id(1)
    ts = qkv_ref.shape[1]
    bw = u0_ref.shape[2]
    nh = bw // HEAD_DIM
    halo = halo_ref[0]
    xx_ref[0:8, :] = jnp.where(i > 0, halo, 0.0)
    xx_ref[8:, :] = qkv_ref[0]

    cw = cw_ref[...]

    def conv_chunk(ci):
        x = xx_ref[ci * CHUNK:ci * CHUNK + CHUNK + 8, :]
        acc = cw[SHORT_CONV - 1:SHORT_CONV, :] * x[8:, :]
        for s in range(1, SHORT_CONV):
            k = SHORT_CONV - 1 - s
            acc = acc + cw[k:k + 1, :] * pltpu.roll(x, s, 0)[8:, :]
        act_ref[ci * CHUNK:(ci + 1) * CHUNK, :] = _silu(acc)

    r = lax.broadcasted_iota(jnp.int32, (CHUNK, CHUNK), 0)
    c = lax.broadcasted_iota(jnp.int32, (CHUNK, CHUNK), 1)
    incl = r >= c
    strict = r > c
    tril = incl.astype(F32)
    triu = (r <= c).astype(F32)
    eye = jnp.where(r == c, 1.0, 0.0).astype(F32)
    nchunks = ts // CHUNK
    cpi = GDN_CHUNKS_PER_ITER if nchunks % GDN_CHUNKS_PER_ITER == 0 else 1

    def iter_body(it, _):
        units = []
        for cc in range(cpi):
            ci = it * cpi + cc
            conv_chunk(ci)
            rows = pl.ds(ci * CHUNK, CHUNK)
            sc = sc_ref[0, rows, :]
            g_col = _dot_f32(tril, sc)
            g_row = _dot_f32(sr_ref[ci], triu)
            for h in range(nh):
                q = act_ref[rows, h * HEAD_DIM:(h + 1) * HEAD_DIM]
                k_ = act_ref[rows, bw + h * HEAD_DIM: bw + (h + 1) * HEAD_DIM]
                v = act_ref[rows, 2 * bw + h * HEAD_DIM: 2 * bw + (h + 1) * HEAD_DIM]
                q = q * (lax.rsqrt(jnp.sum(q * q, axis=-1, keepdims=True) + NORM_EPS) * HEAD_DIM ** -0.5)
                k_ = k_ * lax.rsqrt(jnp.sum(k_ * k_, axis=-1, keepdims=True) + NORM_EPS)
                beta = sc[:, h:h + 1]
                g = g_col[:, nh + h:nh + h + 1]
                g_last = g_col[CHUNK - 1:CHUNK, nh + h:nh + h + 1]
                gr = g_row[nh + h:nh + h + 1, :]
                units.append(dict(
                    ci=ci, rows=rows, h=h, q=q, k=k_, v=v, beta=beta, g=g, g_last=g_last,
                    decay=jnp.exp(jnp.where(incl, g - gr, -jnp.inf)),
                    eg=jnp.exp(g), kb=k_.astype(BF16), qb=q.astype(BF16)))

        for u in units:
            u["kq"] = _dot_nt(jnp.concatenate([u["kb"], u["qb"]], axis=0), u["kb"])
        for u in units:
            u["qk"] = u["kq"][CHUNK:] * u["decay"]
            a = jnp.where(strict, -(u["beta"] * u["kq"][:CHUNK] * u["decay"]), 0.0)
            u["x"] = eye + a
            u["p"] = a
        for u in units:
            pb = u["p"].astype(BF16)
            u["p"] = _dot(pb, pb)
        m = 2
        while 2 * m < CHUNK:
            for u in units:
                pb = u["p"].astype(BF16)
                out = _dot(jnp.concatenate([u["x"].astype(BF16), pb], axis=0), pb)
                u["x"] = u["x"] + out[:CHUNK]
                u["p"] = out[CHUNK:]
            m *= 2
        for u in units:
            u["x"] = u["x"] + _dot(u["x"].astype(BF16), u["p"].astype(BF16))
        for u in units:
            rhs = jnp.concatenate([u["v"] * u["beta"], u["k"] * (u["beta"] * u["eg"])], axis=1)
            u["uw"] = _dot(u["x"].astype(BF16), rhs.astype(BF16))

        for u in units:
            h, rows = u["h"], u["rows"]
            sl = slice(h * HEAD_DIM, (h + 1) * HEAD_DIM)
            u0_ref[0, rows, sl] = u["uw"][:, :HEAD_DIM]
            w_ref[0, rows, sl] = u["uw"][:, HEAD_DIM:].astype(BF16)
            qd_ref[0, rows, sl] = (u["q"] * u["eg"]).astype(BF16)
            kt_ref[0, rows, sl] = (u["k"] * jnp.exp(u["g_last"] - u["g"])).astype(BF16)
            qk_ref[0, rows, h * CHUNK:(h + 1) * CHUNK] = u["qk"].astype(BF16)
            dec_ref[0, u["ci"], :, sl] = jnp.broadcast_to(jnp.exp(u["g_last"]), (1, HEAD_DIM))
        return 0

    for it in range(nchunks // cpi):
        iter_body(it, 0)


def _gdn_local(qkv_a, conv_w, small_c, small_ch, ts):
    bsz, seq, n3 = qkv_a.shape
    bw = n3 // 3
    nh = bw // HEAD_DIM
    nt = seq // ts
    nc = ts // CHUNK
    big = lambda b, i: (b, i, 0)
    return pl.pallas_call(
        _gdn_local_kernel,
        out_shape=(jax.ShapeDtypeStruct((bsz, seq, bw), F32),
                   jax.ShapeDtypeStruct((bsz, seq, bw), BF16),
                   jax.ShapeDtypeStruct((bsz, seq, bw), BF16),
                   jax.ShapeDtypeStruct((bsz, seq, bw), BF16),
                   jax.ShapeDtypeStruct((bsz, seq, nh * CHUNK), BF16),
                   jax.ShapeDtypeStruct((bsz, seq // CHUNK, 1, bw), F32)),
        grid=(bsz, nt),
        in_specs=[pl.BlockSpec((1, ts, n3), big),
                  pl.BlockSpec((1, 8, n3), lambda b, i: (b, jnp.maximum(i * (ts // 8) - 1, 0), 0)),
                  pl.BlockSpec((SHORT_CONV, n3), lambda b, i: (0, 0)),
                  pl.BlockSpec((1, ts, N_SMALL), big),
                  pl.BlockSpec((nc, N_SMALL_T, CHUNK), lambda b, i: (b * nt + i, 0, 0))],
        out_specs=(pl.BlockSpec((1, ts, bw), big),
                   pl.BlockSpec((1, ts, bw), big),
                   pl.BlockSpec((1, ts, bw), big),
                   pl.BlockSpec((1, ts, bw), big),
                   pl.BlockSpec((1, ts, nh * CHUNK), big),
                   pl.BlockSpec((1, nc, 1, bw), lambda b, i: (b, i, 0, 0))),
        scratch_shapes=[pltpu.VMEM((ts + 8, n3), F32), pltpu.VMEM((ts, n3), F32)],
        compiler_params=_cparams(2),
        name="gdn_local",
    )(qkv_a, qkv_a, conv_w, small_c.reshape(bsz, seq, N_SMALL), small_ch)


def _gdn_scan_kernel(u0_ref, w_ref, qd_ref, kt_ref, qk_ref, dec_ref, z_ref, nw_ref, y_ref, state_ref):
    i = pl.program_id(0)
    bsz, ts, bw = u0_ref.shape
    nh = bw // HEAD_DIM

    @pl.when(i == 0)
    def _():
        state_ref[...] = jnp.zeros_like(state_ref)

    nw = nw_ref[...]

    def chunk_body(ci, _):
        rows = pl.ds(pl.multiple_of(ci * CHUNK, CHUNK), CHUNK)
        units = []
        for b in range(bsz):
            for h in range(nh):
                sl = slice(h * HEAD_DIM, (h + 1) * HEAD_DIM)
                state = state_ref[b * nh + h]
                lhs = jnp.concatenate([w_ref[b, rows, sl], qd_ref[b, rows, sl]], axis=0)
                units.append(dict(b=b, h=h, sl=sl, state=state, ws=_dot(lhs, state.astype(BF16))))
        for u in units:
            b, h, sl = u["b"], u["h"], u["sl"]
            ub = (u0_ref[b, rows, sl] - u["ws"][:CHUNK]).astype(BF16)
            u["o"] = u["ws"][CHUNK:] + _dot(qk_ref[b, rows, h * CHUNK:(h + 1) * CHUNK], ub)
            u["ds"] = _dot_tn(kt_ref[b, rows, sl], ub)
        for u in units:
            b, h, sl = u["b"], u["h"], u["sl"]
            state_ref[b * nh + h] = u["state"] * dec_ref[b, ci, :, sl] + u["ds"]
            o = u["o"]
            o = o * lax.rsqrt(jnp.mean(o * o, axis=-1, keepdims=True) + NORM_EPS)
            y_ref[b, rows, sl] = (o * nw * z_ref[b, rows, sl].astype(F32)).astype(BF16)
        return 0

    lax.fori_loop(0, ts // CHUNK, chunk_body, 0)


def _gdn_scan(u0, w, qd, kt, qk, dec, z_all, o_norm_w, ts):
    bsz, seq, bw = u0.shape
    nh = bw // HEAD_DIM
    big = lambda i: (0, i, 0)
    return pl.pallas_call(
        _gdn_scan_kernel,
        out_shape=jax.ShapeDtypeStruct((bsz, seq, bw), BF16),
        grid=(seq // ts,),
        in_specs=[pl.BlockSpec((bsz, ts, bw), big),
                  pl.BlockSpec((bsz, ts, bw), big),
                  pl.BlockSpec((bsz, ts, bw), big),
                  pl.BlockSpec((bsz, ts, bw), big),
                  pl.BlockSpec((bsz, ts, nh * CHUNK), big),
                  pl.BlockSpec((bsz, ts // CHUNK, 1, bw), lambda i: (0, i, 0, 0)),
                  pl.BlockSpec((bsz, ts, bw), big),
                  pl.BlockSpec((1, HEAD_DIM), lambda i: (0, 0))],
        out_specs=pl.BlockSpec((bsz, ts, bw), big),
        scratch_shapes=[pltpu.VMEM((bsz * nh, HEAD_DIM, HEAD_DIM), F32)],
        compiler_params=_cparams(1),
        name="gdn_scan",
    )(u0, w, qd, kt, qk, dec, z_all, o_norm_w.reshape(1, HEAD_DIM))


def _conformer_kernel(u_ref, halo_ref, z_ref, cw_ref, cb_ref, lw_ref, lb_ref, y_ref, xx_ref):
    i = pl.program_id(1)
    ts = u_ref.shape[1]
    pad = halo_ref.shape[1]
    xx_ref[0:pad, :] = jnp.where(i > 0, halo_ref[0], 0.0)
    xx_ref[pad:, :] = u_ref[0]
    cw = cw_ref[...]
    off = pad - (CONF_CONV - 1)
    acc = None
    for r in range(8):
        rows = ts if r == 0 else ts + 8
        y = None
        for m in range((pad + 8) // 8):
            k = 8 * m + r - off
            if 0 <= k < CONF_CONV:
                term = cw[k:k + 1, :] * xx_ref[8 * m:8 * m + rows, :]
                y = term if y is None else y + term
        shifted = y[r:r + ts, :]
        acc = shifted if acc is None else acc + shifted
    u = acc + cb_ref[...]
    xc = u - jnp.mean(u, axis=-1, keepdims=True)
    y = xc * lax.rsqrt(jnp.mean(xc * xc, axis=-1, keepdims=True) + NORM_EPS)
    y = _silu(y * lw_ref[...] + lb_ref[...])
    y_ref[0] = (y * z_ref[0].astype(F32)).astype(BF16)


def _conformer(u, z_all, conv_w, conv_b, ln_w, ln_b, ts):
    bsz, seq, bw = u.shape
    pad = 32
    big = lambda b, i: (b, i, 0)
    vec = lambda b, i: (0, 0)
    return pl.pallas_call(
        _conformer_kernel,
        out_shape=jax.ShapeDtypeStruct((bsz, seq, bw), BF16),
        grid=(bsz, seq // ts),
        in_specs=[pl.BlockSpec((1, ts, bw), big),
                  pl.BlockSpec((1, pad, bw), lambda b, i: (b, jnp.maximum(i * (ts // pad) - 1, 0), 0)),
                  pl.BlockSpec((1, ts, bw), lambda b, i: (b, i, 1)),
                  pl.BlockSpec((CONF_CONV, bw), vec),
                  pl.BlockSpec((1, bw), vec),
                  pl.BlockSpec((1, bw), vec),
                  pl.BlockSpec((1, bw), vec)],
        out_specs=pl.BlockSpec((1, ts, bw), big),
        scratch_shapes=[pltpu.VMEM((ts + pad, bw), F32)],
        compiler_params=_cparams(2),
        name="conformer",
    )(u, u, z_all, conv_w, conv_b.reshape(1, bw), ln_w.reshape(1, bw), ln_b.reshape(1, bw))


FOX_SUB = 256
FOX_HEADS = 4


def _fox_kernel(qi_ref, kj_ref, qt_ref, k_ref, vt_ref, c_ref, z_ref, y_ref, m_ref, l_ref, acc_ref):
    t = pl.program_id(2)
    i = qi_ref[t]
    j = kj_ref[t]
    tq = qt_ref.shape[2]
    tk = k_ref.shape[1]
    hp = c_ref.shape[1]
    nsub = tq // FOX_SUB
    streams = [(g, n) for g in range(hp) for n in range(nsub)]

    @pl.when(j == 0)
    def _():
        m_ref[...] = jnp.full_like(m_ref, -jnp.inf)
        l_ref[...] = jnp.zeros_like(l_ref)
        acc_ref[...] = jnp.zeros_like(acc_ref)

    gate_rows = lax.broadcasted_iota(jnp.int32, (c_ref.shape[3], FOX_SUB), 0)
    minus_gate = jnp.where(gate_rows < N_SPLIT, -1.0, 0.0).astype(BF16)

    def step(diag):
        rows = [min((n + 1) * FOX_SUB, tk) if diag else tk for n in range(nsub)]
        s_all = []
        for g, n in streams:
            hd = slice(g * HEAD_DIM, (g + 1) * HEAD_DIM)
            k_aug = jnp.concatenate([k_ref[0, :rows[n], hd], c_ref[0, g, :rows[n], :]], axis=1)
            q_aug = jnp.concatenate([qt_ref[0, hd, n * FOX_SUB:(n + 1) * FOX_SUB], minus_gate], axis=0)
            s_all.append(_dot(k_aug, q_aug))
        for (g, n), s in zip(streams, s_all):
            hd = slice(g * HEAD_DIM, (g + 1) * HEAD_DIM)
            cols = slice(n * FOX_SUB, (n + 1) * FOX_SUB)
            if diag:
                kpos = lax.broadcasted_iota(jnp.int32, s.shape, 0)
                qpos = lax.broadcasted_iota(jnp.int32, s.shape, 1) + n * FOX_SUB
                s = jnp.where(kpos <= qpos, s, -jnp.inf)
            m_prev = m_ref[g:g + 1, cols]
            m_new = jnp.maximum(m_prev, jnp.max(s, axis=0, keepdims=True))
            p = jnp.exp2(s - m_new)
            alpha = jnp.exp2(m_prev - m_new)
            l_ref[g:g + 1, cols] = alpha * l_ref[g:g + 1, cols] + jnp.sum(p, axis=0, keepdims=True)
            acc_ref[hd, cols] = alpha * acc_ref[hd, cols] + _dot(vt_ref[0, hd, :rows[n]], p.astype(BF16))
            m_ref[g:g + 1, cols] = m_new

    @pl.when(j < i)
    def _():
        step(False)

    @pl.when(j == i)
    def _():
        step(True)
        for g in range(hp):
            hd = slice(g * HEAD_DIM, (g + 1) * HEAD_DIM)
            o = (acc_ref[hd, :] / l_ref[g:g + 1, :]).T
            y_ref[0, :, hd] = (o * z_ref[0, :, hd].astype(F32)).astype(BF16)


def _fox_attention(q_t, k, v_t, c_split, z_all, t):
    bsz, bw, seq = q_t.shape
    nh = bw // HEAD_DIM
    hp = FOX_HEADS if nh % FOX_HEADS == 0 else 1
    hw = hp * HEAD_DIM
    nt = seq // t
    steps = [(i, j) for i in range(nt) for j in range(i + 1)]
    qi = jnp.asarray([s[0] for s in steps], jnp.int32)
    kj = jnp.asarray([s[1] for s in steps], jnp.int32)
    grid_spec = pltpu.PrefetchScalarGridSpec(
        num_scalar_prefetch=2,
        grid=(bsz, nh // hp, len(steps)),
        in_specs=[pl.BlockSpec((1, hw, t), lambda b, h, s, qi, kj: (b, h, qi[s])),
                  pl.BlockSpec((1, t, hw), lambda b, h, s, qi, kj: (b, kj[s], h)),
                  pl.BlockSpec((1, hw, t), lambda b, h, s, qi, kj: (b, h, kj[s])),
                  pl.BlockSpec((1, hp, t, 128), lambda b, h, s, qi, kj: (b, h, kj[s], 0)),
                  pl.BlockSpec((1, t, hw), lambda b, h, s, qi, kj: (b, qi[s], 2 * (nh // hp) + h))],
        out_specs=pl.BlockSpec((1, t, hw), lambda b, h, s, qi, kj: (b, qi[s], h)),
        scratch_shapes=[pltpu.VMEM((hp, t), F32), pltpu.VMEM((hp, t), F32), pltpu.VMEM((hw, t), F32)],
    )
    return pl.pallas_call(
        _fox_kernel,
        out_shape=jax.ShapeDtypeStruct((bsz, seq, bw), BF16),
        grid_spec=grid_spec,
        compiler_params=_cparams(3),
        name="fox_attention",
    )(qi, kj, q_t, k, v_t, c_split, z_all)


def _merge_kernel(h_ref, ya_ref, yb_ref, yc_ref, wg0_ref, wg1_ref, wg2_ref, wb0_ref, wb1_ref, wb2_ref, o_ref):
    h = h_ref[...]
    acc = None
    for y_ref, wg_ref, wb_ref in ((ya_ref, wg0_ref, wb0_ref), (yb_ref, wg1_ref, wb1_ref), (yc_ref, wg2_ref, wb2_ref)):
        term = _sigmoid(_dot_nt(h, wg_ref[0])) * _dot(y_ref[...], wb_ref[0, 0])
        acc = term if acc is None else acc + term
    o_ref[...] = acc.astype(BF16)


def _merge(h, ya, yb, yc, w_packed, gate_col0, w_branch, layer, tm, tn):
    m, d = h.shape
    bw = ya.shape[1]
    nj = d // tn
    g0 = gate_col0 // tn
    row = lambda j, i: (i, 0)
    return pl.pallas_call(
        _merge_kernel,
        out_shape=jax.ShapeDtypeStruct((m, d), BF16),
        grid=(nj, m // tm),
        in_specs=[pl.BlockSpec((tm, d), row),
                  pl.BlockSpec((tm, bw), row), pl.BlockSpec((tm, bw), row), pl.BlockSpec((tm, bw), row),
                  pl.BlockSpec((1, tn, d), lambda j, i: (layer, g0 + j, 0)),
                  pl.BlockSpec((1, tn, d), lambda j, i: (layer, g0 + nj + j, 0)),
                  pl.BlockSpec((1, tn, d), lambda j, i: (layer, g0 + 2 * nj + j, 0)),
                  pl.BlockSpec((1, 1, bw, tn), lambda j, i: (layer, 0, 0, j)),
                  pl.BlockSpec((1, 1, bw, tn), lambda j, i: (layer, 1, 0, j)),
                  pl.BlockSpec((1, 1, bw, tn), lambda j, i: (layer, 2, 0, j))],
        out_specs=pl.BlockSpec((tm, tn), lambda j, i: (i, j)),
        compiler_params=_cparams(2),
        name="gated_merge",
    )(h, ya, yb, yc, w_packed, w_packed, w_packed, w_branch, w_branch, w_branch)


def _out_kernel(m_ref, w_ref, x_ref, pw_ref, *rest):
    out = _dot(m_ref[...], w_ref[0])
    y = out * lax.rsqrt(jnp.mean(out * out, axis=-1, keepdims=True) + NORM_EPS)
    x_new = x_ref[...] + y * pw_ref[...]
    if len(rest) == 1:
        (o_ref,) = rest
    else:
        nw_ref, o_ref, h_ref = rest
        hn = x_new * lax.rsqrt(jnp.mean(x_new * x_new, axis=-1, keepdims=True) + NORM_EPS)
        h_ref[...] = (hn * nw_ref[...]).astype(BF16)
    o_ref[...] = x_new


def _out_proj(merged, w_out, layer, x2, post_w, next_pre_w, tm):
    m, d = x2.shape
    row = pl.BlockSpec((tm, d), lambda i: (i, 0))
    vec = pl.BlockSpec((1, d), lambda i: (0, 0))
    in_specs = [row, pl.BlockSpec((1, d, d), lambda i: (layer, 0, 0)), row, vec]
    args = [merged, w_out, x2, post_w.reshape(1, d)]
    out_shape, out_specs = jax.ShapeDtypeStruct((m, d), F32), row
    if next_pre_w is not None:
        in_specs.append(vec)
        args.append(next_pre_w.reshape(1, d))
        out_shape, out_specs = (out_shape, jax.ShapeDtypeStruct((m, d), BF16)), (row, row)
    return pl.pallas_call(
        _out_kernel,
        out_shape=out_shape,
        grid=(m // tm,),
        in_specs=in_specs,
        out_specs=out_specs,
        compiler_params=_cparams(1),
        name="out_proj",
    )(*args)


def _layer(x2, h, bsz, seq, l, shared, p, next_pre_w):
    m, d = x2.shape
    bw = d // 2
    tm = min(1024, m)

    wp = shared["w_packed"]
    col = _packed_columns(d)
    qkv_a = _proj(h, wp, l, (col["qkv_a"],), 3 * bw, epilogue="f32", tm=tm, tn=1024, out_dtype=F32)
    z_all = _proj(h, wp, l, (col["z_a"], col["z_b"], col["z_c"]), bw, epilogue="silu_bf16", tm=tm, tn=1024,
                  out_dtype=BF16)
    u_b = _proj_glu(h, wp, l, col["glu"], bw, tm=tm, tn=512)
    q_t = _proj_t(h, wp, l, col["q_c"], bw, bsz=bsz, tm=min(1024, seq), tn=1024,
                  scale=HEAD_DIM ** -0.5 * LOG2_E)
    k_c = _proj(h, wp, l, (col["k_c"],), bw, epilogue="bf16", tm=tm, tn=1024, out_dtype=BF16)
    v_t = _proj_t(h, wp, l, col["v_c"], bw, bsz=bsz, tm=min(1024, seq), tn=1024)
    small_c, small_ch = _small_proj(h, shared["w_small"], shared["prm_c"], l, min(512, seq))

    z3 = z_all.reshape(bsz, seq, 3 * bw)
    u0, w, qd, kt, qk, dec = _gdn_local(qkv_a.reshape(bsz, seq, 3 * bw), p["conv_qkv_w"], small_c, small_ch,
                                        min(256, seq))
    y_a = _gdn_scan(u0, w, qd, kt, qk, dec, z3, p["o_norm_w"], min(512, seq))

    y_b = _conformer(u_b.reshape(bsz, seq, bw), z3, p["conv_w"], p["conv_b"], p["ln_w"], p["ln_b"], min(256, seq))

    c_split = _forget_cumsum(small_c.reshape(bsz, seq, N_SMALL), bw // HEAD_DIM, min(1024, seq))
    y_c = _fox_attention(q_t, k_c.reshape(bsz, seq, bw), v_t, c_split, z3, min(1024, seq))

    merged = _merge(h, y_a.reshape(m, bw), y_b.reshape(m, bw), y_c.reshape(m, bw),
                    wp, col["gate"], shared["w_branch"], l, min(512, m), 512)
    out = _out_proj(merged, shared["w_out"], l, x2, p["post_w"], next_pre_w, min(256, m))
    return out if next_pre_w is not None else (out, None)


REPACK_TR = 512


def _packed_columns(d):
    bw = d // 2
    names = (("qkv_a", 3 * bw), ("z_a", bw), ("glu", 2 * bw), ("z_b", bw), ("q_c", bw), ("k_c", bw),
             ("v_c", bw), ("z_c", bw), ("gate", N_BRANCH * d))
    col, off = {}, 0
    for name, width in names:
        col[name] = off
        off += width
    col["total"] = off
    return col


def _repack_kernel(x_ref, o_ref):
    o_ref[...] = x_ref[...].astype(BF16)


def _repack_w_in(w_t, nh):
    depth, n_in, d = w_t.shape
    col = _packed_columns(d)
    tr = REPACK_TR
    t_glu, t_gate = col["glu"] // tr, col["gate"] // tr
    assert col["glu"] % tr == 0 and col["gate"] % tr == 0 and col["total"] % tr == 0
    assert col["total"] + 3 * nh == n_in

    def src_row(t):
        row = t * tr + jnp.where(t >= t_gate, 3 * nh, jnp.where(t >= t_glu, 2 * nh, 0))
        return pl.multiple_of(row, nh)
    return pl.pallas_call(
        _repack_kernel,
        out_shape=jax.ShapeDtypeStruct((depth, col["total"], d), BF16),
        grid=(depth, col["total"] // tr),
        in_specs=[pl.BlockSpec((pl.Element(1), pl.Element(tr), pl.Element(d)), lambda l, t: (l, src_row(t), 0))],
        out_specs=pl.BlockSpec((1, tr, d), lambda l, t: (l, t, 0)),
        compiler_params=_cparams(2),
        name="repack_w_in",
    )(w_t)


def _small_w_kernel(x_ref, y_ref, o_ref):
    o_ref[...] = jnp.zeros_like(o_ref)
    nx, ny = x_ref.shape[1], y_ref.shape[1]
    o_ref[0, 0:nx, :] = x_ref[0]
    o_ref[0, nx:nx + ny, :] = y_ref[0]


def _small_w(w_t, nh):
    depth, _, d = w_t.shape
    bw = d // 2
    beta0 = 4 * bw
    forget0 = beta0 + 2 * nh + 7 * bw
    return pl.pallas_call(
        _small_w_kernel,
        out_shape=jax.ShapeDtypeStruct((depth, N_SMALL, d), F32),
        grid=(depth,),
        in_specs=[pl.BlockSpec((pl.Element(1), pl.Element(2 * nh), pl.Element(d)), lambda l: (l, beta0, 0)),
                  pl.BlockSpec((pl.Element(1), pl.Element(nh), pl.Element(d)), lambda l: (l, forget0, 0))],
        out_specs=pl.BlockSpec((1, N_SMALL, d), lambda l: (l, 0, 0)),
        compiler_params=_cparams(1),
        name="small_w",
    )(w_t, w_t)


def _prep_params(pre_norm_w, post_norm_w, w_in, conv_qkv_w, a_log, dt_bias, o_norm_w,
                 conv_w, conv_b, ln_w, ln_b, f_bias, w_branch, w_out):
    depth, d, _ = w_in.shape
    bw = d // 2
    nh = bw // HEAD_DIM
    w_t = jnp.swapaxes(w_in, 1, 2)
    w_packed = _repack_w_in(w_t, nh)
    w_small = _small_w(w_t, nh)
    zeros = jnp.zeros((depth, nh), F32)
    ones = jnp.ones((depth, nh), F32)
    vec24 = lambda a, b, c: jnp.concatenate([a, b, c], axis=1)
    prm = jnp.stack([vec24(ones, jnp.exp(a_log.astype(F32)), ones),
                     vec24(zeros, dt_bias.astype(F32), zeros),
                     vec24(zeros, zeros, f_bias.astype(F32))], axis=1)
    prm_c = jnp.pad(prm, ((0, 0), (0, 8 - 3), (0, N_SMALL - 3 * nh)))
    shared = {"w_packed": w_packed, "w_small": w_small, "prm_c": prm_c,
              "w_branch": w_branch.astype(BF16), "w_out": w_out.astype(BF16)}
    per_layer = {"post_w": post_norm_w, "conv_qkv_w": conv_qkv_w, "o_norm_w": o_norm_w,
                 "conv_w": conv_w, "conv_b": conv_b, "ln_w": ln_w, "ln_b": ln_b}
    return shared, per_layer


def kernel(x, pre_norm_w, post_norm_w, w_in, conv_qkv_w, a_log, dt_bias, o_norm_w, conv_w, conv_b, ln_w, ln_b, f_bias, w_branch, w_out):
    bsz, seq, d = x.shape
    depth = w_in.shape[0]
    shared, per_layer = _prep_params(pre_norm_w, post_norm_w, w_in, conv_qkv_w, a_log, dt_bias, o_norm_w,
                                     conv_w, conv_b, ln_w, ln_b, f_bias, w_branch, w_out)
    x2 = x.reshape(bsz * seq, d)
    h = _prenorm(x2, pre_norm_w[0], min(512, bsz * seq))
    for l in range(depth):
        next_pre_w = pre_norm_w[l + 1] if l + 1 < depth else None
        x2, h = _layer(x2, h, bsz, seq, l, shared, {k: v[l] for k, v in per_layer.items()}, next_pre_w)
    return x2.reshape(bsz, seq, d)
```

```python
import functools

import jax
import jax.numpy as jnp
from jax import lax
from jax.experimental import pallas as pl
from jax.experimental.pallas import tpu as pltpu

F32 = jnp.float32
BF16 = jnp.bfloat16

HEAD_DIM = 128
CHUNK = 64
SHORT_CONV = 4
CONF_CONV = 31
NORM_EPS = 1e-6
N_BRANCH = 3
N_SMALL = 128
N_SMALL_T = 32
V7X_VMEM_LIMIT = 48 * 1024 * 1024


def _cparams(n_axes):
    return pltpu.CompilerParams(dimension_semantics=("arbitrary",) * n_axes,
                                vmem_limit_bytes=V7X_VMEM_LIMIT)


def _sigmoid(x):
    return 1.0 / (1.0 + jnp.exp(-x))


def _silu(x):
    return x * _sigmoid(x)


def _softplus(x):
    return jnp.maximum(x, 0.0) + jnp.log(1.0 + jnp.exp(-jnp.abs(x)))


def _dot(a, b):
    return jnp.dot(a, b, preferred_element_type=F32)


def _dot_nt(a, b):
    return lax.dot_general(a, b, (((1,), (1,)), ((), ())), preferred_element_type=F32)


def _dot_tn(a, b):
    return lax.dot_general(a, b, (((0,), (0,)), ((), ())), preferred_element_type=F32)


def _dot_f32(a, b):
    return jnp.dot(a, b, preferred_element_type=F32, precision=lax.Precision.HIGHEST)


def _prenorm_kernel(x_ref, w_ref, h_ref):
    x = x_ref[...]
    y = x * lax.rsqrt(jnp.mean(x * x, axis=-1, keepdims=True) + NORM_EPS)
    h_ref[...] = (y * w_ref[...]).astype(BF16)


def _prenorm(x2, w, tm):
    m, d = x2.shape
    return pl.pallas_call(
        _prenorm_kernel,
        out_shape=jax.ShapeDtypeStruct((m, d), BF16),
        grid=(m // tm,),
        in_specs=[pl.BlockSpec((tm, d), lambda i: (i, 0)),
                  pl.BlockSpec((1, d), lambda i: (0, 0))],
        out_specs=pl.BlockSpec((tm, d), lambda i: (i, 0)),
        compiler_params=_cparams(1),
        name="prenorm",
    )(x2, w.reshape(1, d))


def _proj_kernel(h_ref, w_ref, o_ref, *, epilogue):
    acc = _dot_nt(h_ref[...], w_ref[0])
    if epilogue == "f32":
        o_ref[...] = acc
    elif epilogue == "silu_bf16":
        o_ref[...] = _silu(acc).astype(BF16)
    elif epilogue == "bf16":
        o_ref[...] = acc.astype(BF16)
    else:
        raise ValueError(epilogue)


def _proj(h, w_packed, layer, seg_cols, seg_width, *, epilogue, tm, tn, out_dtype):
    m, k = h.shape
    n = len(seg_cols) * seg_width
    per_seg = seg_width // tn

    def w_tile(j):
        tile = seg_cols[0] // tn + j
        for s in range(1, len(seg_cols)):
            tile = jnp.where(j >= s * per_seg, seg_cols[s] // tn + j - s * per_seg, tile)
        return tile
    return pl.pallas_call(
        functools.partial(_proj_kernel, epilogue=epilogue),
        out_shape=jax.ShapeDtypeStruct((m, n), out_dtype),
        grid=(n // tn, m // tm),
        in_specs=[pl.BlockSpec((tm, k), lambda j, i: (i, 0)),
                  pl.BlockSpec((1, tn, k), lambda j, i: (layer, w_tile(j), 0))],
        out_specs=pl.BlockSpec((tm, tn), lambda j, i: (i, j)),
        compiler_params=_cparams(2),
        name="proj_" + epilogue,
    )(h, w_packed)


def _proj_t_kernel(wt_ref, h_ref, o_ref, *, scale):
    o_ref[0] = (_dot_nt(wt_ref[0], h_ref[...]) * scale).astype(BF16)


def _proj_t(h, w_packed, layer, row0, n, *, bsz, tm, tn, scale=1.0):
    m, k = h.shape
    seq = m // bsz
    nts = seq // tm
    return pl.pallas_call(
        functools.partial(_proj_t_kernel, scale=scale),
        out_shape=jax.ShapeDtypeStruct((bsz, n, seq), BF16),
        grid=(n // tn, m // tm),
        in_specs=[pl.BlockSpec((1, tn, k), lambda j, i: (layer, row0 // tn + j, 0)),
                  pl.BlockSpec((tm, k), lambda j, i: (i, 0))],
        out_specs=pl.BlockSpec((1, tn, tm), lambda j, i: (i // nts, j, i % nts)),
        compiler_params=_cparams(2),
        name="proj_t",
    )(w_packed, h)


def _glu_kernel(h_ref, wv_ref, wg_ref, o_ref):
    h = h_ref[...]
    o_ref[...] = _dot_nt(h, wv_ref[0]) * _sigmoid(_dot_nt(h, wg_ref[0]))


def _proj_glu(h, w_packed, layer, col0, n, *, tm, tn):
    m, k = h.shape
    return pl.pallas_call(
        _glu_kernel,
        out_shape=jax.ShapeDtypeStruct((m, n), F32),
        grid=(n // tn, m // tm),
        in_specs=[pl.BlockSpec((tm, k), lambda j, i: (i, 0)),
                  pl.BlockSpec((1, tn, k), lambda j, i: (layer, col0 // tn + j, 0)),
                  pl.BlockSpec((1, tn, k), lambda j, i: (layer, (col0 + n) // tn + j, 0))],
        out_specs=pl.BlockSpec((tm, tn), lambda j, i: (i, j)),
        compiler_params=_cparams(2),
        name="proj_glu",
    )(h, w_packed, w_packed)


def _gate_transform(x, exp_a, dt_bias, f_bias, idx):
    beta = _sigmoid(x)
    log_a = -exp_a * _softplus(x + dt_bias)
    log_f = -_softplus(-(x + f_bias))
    return jnp.where(idx < 8, beta, jnp.where(idx < 16, log_a, log_f))


def _small_kernel(h_ref, w_ref, pc_ref, o_ref, oc_ref):
    col = _dot_nt(h_ref[...], w_ref[0].astype(BF16))
    pc = pc_ref[0]
    cidx = lax.broadcasted_iota(jnp.int32, col.shape, 1)
    col = _gate_transform(col, pc[0:1, :], pc[1:2, :], pc[2:3, :], cidx)
    o_ref[...] = col
    row = col.T
    for c in range(row.shape[1] // CHUNK):
        oc_ref[c] = row[:N_SMALL_T, c * CHUNK:(c + 1) * CHUNK]


def _small_proj(h, w_small, prm_c, layer, tm):
    m, k = h.shape
    return pl.pallas_call(
        _small_kernel,
        out_shape=(jax.ShapeDtypeStruct((m, N_SMALL), F32),
                   jax.ShapeDtypeStruct((m // CHUNK, N_SMALL_T, CHUNK), F32)),
        grid=(m // tm,),
        in_specs=[pl.BlockSpec((tm, k), lambda i: (i, 0)),
                  pl.BlockSpec((1, N_SMALL, k), lambda i: (layer, 0, 0)),
                  pl.BlockSpec((1, 8, N_SMALL), lambda i: (layer, 0, 0))],
        out_specs=(pl.BlockSpec((tm, N_SMALL), lambda i: (i, 0)),
                   pl.BlockSpec((tm // CHUNK, N_SMALL_T, CHUNK), lambda i: (i, 0, 0))),
        compiler_params=_cparams(1),
        name="small_proj",
    )(h, w_small, prm_c)


def _split3(x):
    a = x.astype(BF16)
    r1 = x - a.astype(F32)
    b = r1.astype(BF16)
    c = (r1 - b.astype(F32)).astype(BF16)
    return a, b, c


LOG2_E = 1.4426950408889634
N_SPLIT = 3


def _cumsum_kernel(x_ref, o_ref, carry_ref):
    ts = x_ref.shape[1]
    nh = o_ref.shape[1]
    blk = 128

    @pl.when(pl.program_id(1) == 0)
    def _():
        carry_ref[...] = jnp.zeros_like(carry_ref)

    r = lax.broadcasted_iota(jnp.int32, (blk, blk), 0)
    c = lax.broadcasted_iota(jnp.int32, (blk, blk), 1)
    tril = jnp.where(r >= c, 1.0, 0.0).astype(BF16)
    er = lax.broadcasted_iota(jnp.int32, (blk, nh * blk), 0)
    ec = lax.broadcasted_iota(jnp.int32, (blk, nh * blk), 1)
    place = [jnp.where((er == 2 * nh + ec // blk) & (ec % blk == t), 1.0, 0.0).astype(BF16)
             for t in range(N_SPLIT)]

    carry = carry_ref[0:1, :]
    for j in range(ts // blk):
        rows = slice(j * blk, (j + 1) * blk)
        x1, x2, x3 = _split3(x_ref[0, rows, :])
        run = _dot(tril, x1) + _dot(tril, x2) + _dot(tril, x3) + carry
        carry = run[blk - 1:blk, :]
        parts = _split3(run * LOG2_E)
        rep = _dot(parts[0], place[0]) + _dot(parts[1], place[1]) + _dot(parts[2], place[2])
        for h in range(nh):
            o_ref[0, h, rows, :] = rep[:, h * blk:(h + 1) * blk].astype(BF16)
    carry_ref[0:1, :] = carry


def _forget_cumsum(small_c3, nh, ts):
    bsz, seq, _ = small_c3.shape
    return pl.pallas_call(
        _cumsum_kernel,
        out_shape=jax.ShapeDtypeStruct((bsz, nh, seq, 128), BF16),
        grid=(bsz, seq // ts),
        in_specs=[pl.BlockSpec((1, ts, N_SMALL), lambda b, i: (b, i, 0))],
        out_specs=pl.BlockSpec((1, nh, ts, 128), lambda b, i: (b, 0, i, 0)),
        scratch_shapes=[pltpu.VMEM((8, N_SMALL), F32)],
        compiler_params=_cparams(2),
        name="forget_cumsum",
    )(small_c3)


GDN_CHUNKS_PER_ITER = 2


def _gdn_local_kernel(qkv_ref, halo_ref, cw_ref, sc_ref, sr_ref,
                      u0_ref, w_ref, qd_ref, kt_ref, qk_ref, dec_ref, xx_ref, act_ref):
    i = pl.program_id(1)
    ts = qkv_ref.shape[1]
    bw = u0_ref.shape[2]
    nh = bw // HEAD_DIM
    halo = halo_ref[0]
    xx_ref[0:8, :] = jnp.where(i > 0, halo, 0.0)
    xx_ref[8:, :] = qkv_ref[0]

    cw = cw_ref[...]

    def conv_chunk(ci):
        x = xx_ref[ci * CHUNK:ci * CHUNK + CHUNK + 8, :]
        acc = cw[SHORT_CONV - 1:SHORT_CONV, :] * x[8:, :]
        for s in range(1, SHORT_CONV):
            k = SHORT_CONV - 1 - s
            acc = acc + cw[k:k + 1, :] * pltpu.roll(x, s, 0)[8:, :]
        act_ref[ci * CHUNK:(ci + 1) * CHUNK, :] = _silu(acc)

    r = lax.broadcasted_iota(jnp.int32, (CHUNK, CHUNK), 0)
    c = lax.broadcasted_iota(jnp.int32, (CHUNK, CHUNK), 1)
    incl = r >= c
    strict = r > c
    tril = incl.astype(F32)
    triu = (r <= c).astype(F32)
    eye = jnp.where(r == c, 1.0, 0.0).astype(F32)
    nchunks = ts // CHUNK
    cpi = GDN_CHUNKS_PER_ITER if nchunks % GDN_CHUNKS_PER_ITER == 0 else 1

    def iter_body(it, _):
        units = []
        for cc in range(cpi):
            ci = it * cpi + cc
            conv_chunk(ci)
            rows = pl.ds(ci * CHUNK, CHUNK)
            sc = sc_ref[0, rows, :]
            g_col = _dot_f32(tril, sc)
            g_row = _dot_f32(sr_ref[ci], triu)
            for h in range(nh):
                q = act_ref[rows, h * HEAD_DIM:(h + 1) * HEAD_DIM]
                k_ = act_ref[rows, bw + h * HEAD_DIM: bw + (h + 1) * HEAD_DIM]
                v = act_ref[rows, 2 * bw + h * HEAD_DIM: 2 * bw + (h + 1) * HEAD_DIM]
                q = q * (lax.rsqrt(jnp.sum(q * q, axis=-1, keepdims=True) + NORM_EPS) * HEAD_DIM ** -0.5)
                k_ = k_ * lax.rsqrt(jnp.sum(k_ * k_, axis=-1, keepdims=True) + NORM_EPS)
                beta = sc[:, h:h + 1]
                g = g_col[:, nh + h:nh + h + 1]
                g_last = g_col[CHUNK - 1:CHUNK, nh + h:nh + h + 1]
                gr = g_row[nh + h:nh + h + 1, :]
                units.append(dict(
                    ci=ci, rows=rows, h=h, q=q, k=k_, v=v, beta=beta, g=g, g_last=g_last,
                    decay=jnp.exp(jnp.where(incl, g - gr, -jnp.inf)),
                    eg=jnp.exp(g), kb=k_.astype(BF16), qb=q.astype(BF16)))

        for u in units:
            u["kq"] = _dot_nt(jnp.concatenate([u["kb"], u["qb"]], axis=0), u["kb"])
        for u in units:
            u["qk"] = u["kq"][CHUNK:] * u["decay"]
            a = jnp.where(strict, -(u["beta"] * u["kq"][:CHUNK] * u["decay"]), 0.0)
            u["x"] = eye + a
            u["p"] = a
        for u in units:
            pb = u["p"].astype(BF16)
            u["p"] = _dot(pb, pb)
        m = 2
        while 2 * m < CHUNK:
            for u in units:
                pb = u["p"].astype(BF16)
                out = _dot(jnp.concatenate([u["x"].astype(BF16), pb], axis=0), pb)
                u["x"] = u["x"] + out[:CHUNK]
                u["p"] = out[CHUNK:]
            m *= 2
        for u in units:
            u["x"] = u["x"] + _dot(u["x"].astype(BF16), u["p"].astype(BF16))
        for u in units:
            rhs = jnp.concatenate([u["v"] * u["beta"], u["k"] * (u["beta"] * u["eg"])], axis=1)
            u["uw"] = _dot(u["x"].astype(BF16), rhs.astype(BF16))

        for u in units:
            h, rows = u["h"], u["rows"]
            sl = slice(h * HEAD_DIM, (h + 1) * HEAD_DIM)
            u0_ref[0, rows, sl] = u["uw"][:, :HEAD_DIM]
            w_ref[0, rows, sl] = u["uw"][:, HEAD_DIM:].astype(BF16)
            qd_ref[0, rows, sl] = (u["q"] * u["eg"]).astype(BF16)
            kt_ref[0, rows, sl] = (u["k"] * jnp.exp(u["g_last"] - u["g"])).astype(BF16)
            qk_ref[0, rows, h * CHUNK:(h + 1) * CHUNK] = u["qk"].astype(BF16)
            dec_ref[0, u["ci"], :, sl] = jnp.broadcast_to(jnp.exp(u["g_last"]), (1, HEAD_DIM))
        return 0

    for it in range(nchunks // cpi):
        iter_body(it, 0)


def _gdn_local(qkv_a, conv_w, small_c, small_ch, ts):
    bsz, seq, n3 = qkv_a.shape
    bw = n3 // 3
    nh = bw // HEAD_DIM
    nt = seq // ts
    nc = ts // CHUNK
    big = lambda b, i: (b, i, 0)
    return pl.pallas_call(
        _gdn_local_kernel,
        out_shape=(jax.ShapeDtypeStruct((bsz, seq, bw), F32),
                   jax.ShapeDtypeStruct((bsz, seq, bw), BF16),
                   jax.ShapeDtypeStruct((bsz, seq, bw), BF16),
                   jax.ShapeDtypeStruct((bsz, seq, bw), BF16),
                   jax.ShapeDtypeStruct((bsz, seq, nh * CHUNK), BF16),
                   jax.ShapeDtypeStruct((bsz, seq // CHUNK, 1, bw), F32)),
        grid=(bsz, nt),
        in_specs=[pl.BlockSpec((1, ts, n3), big),
                  pl.BlockSpec((1, 8, n3), lambda b, i: (b, jnp.maximum(i * (ts // 8) - 1, 0), 0)),
                  pl.BlockSpec((SHORT_CONV, n3), lambda b, i: (0, 0)),
                  pl.BlockSpec((1, ts, N_SMALL), big),
                  pl.BlockSpec((nc, N_SMALL_T, CHUNK), lambda b, i: (b * nt + i, 0, 0))],
        out_specs=(pl.BlockSpec((1, ts, bw), big),
                   pl.BlockSpec((1, ts, bw), big),
                   pl.BlockSpec((1, ts, bw), big),
                   pl.BlockSpec((1, ts, bw), big),
                   pl.BlockSpec((1, ts, nh * CHUNK), big),
                   pl.BlockSpec((1, nc, 1, bw), lambda b, i: (b, i, 0, 0))),
        scratch_shapes=[pltpu.VMEM((ts + 8, n3), F32), pltpu.VMEM((ts, n3), F32)],
        compiler_params=_cparams(2),
        name="gdn_local",
    )(qkv_a, qkv_a, conv_w, small_c.reshape(bsz, seq, N_SMALL), small_ch)


def _gdn_scan_kernel(u0_ref, w_ref, qd_ref, kt_ref, qk_ref, dec_ref, z_ref, nw_ref, y_ref, state_ref):
    i = pl.program_id(0)
    bsz, ts, bw = u0_ref.shape
    nh = bw // HEAD_DIM

    @pl.when(i == 0)
    def _():
        state_ref[...] = jnp.zeros_like(state_ref)

    nw = nw_ref[...]

    def chunk_body(ci, _):
        rows = pl.ds(pl.multiple_of(ci * CHUNK, CHUNK), CHUNK)
        units = []
        for b in range(bsz):
            for h in range(nh):
                sl = slice(h * HEAD_DIM, (h + 1) * HEAD_DIM)
                state = state_ref[b * nh + h]
                lhs = jnp.concatenate([w_ref[b, rows, sl], qd_ref[b, rows, sl]], axis=0)
                units.append(dict(b=b, h=h, sl=sl, state=state, ws=_dot(lhs, state.astype(BF16))))
        for u in units:
            b, h, sl = u["b"], u["h"], u["sl"]
            ub = (u0_ref[b, rows, sl] - u["ws"][:CHUNK]).astype(BF16)
            u["o"] = u["ws"][CHUNK:] + _dot(qk_ref[b, rows, h * CHUNK:(h + 1) * CHUNK], ub)
            u["ds"] = _dot_tn(kt_ref[b, rows, sl], ub)
        for u in units:
            b, h, sl = u["b"], u["h"], u["sl"]
            state_ref[b * nh + h] = u["state"] * dec_ref[b, ci, :, sl] + u["ds"]
            o = u["o"]
            o = o * lax.rsqrt(jnp.mean(o * o, axis=-1, keepdims=True) + NORM_EPS)
            y_ref[b, rows, sl] = (o * nw * z_ref[b, rows, sl].astype(F32)).astype(BF16)
        return 0

    lax.fori_loop(0, ts // CHUNK, chunk_body, 0)


def _gdn_scan(u0, w, qd, kt, qk, dec, z_all, o_norm_w, ts):
    bsz, seq, bw = u0.shape
    nh = bw // HEAD_DIM
    big = lambda i: (0, i, 0)
    return pl.pallas_call(
        _gdn_scan_kernel,
        out_shape=jax.ShapeDtypeStruct((bsz, seq, bw), BF16),
        grid=(seq // ts,),
        in_specs=[pl.BlockSpec((bsz, ts, bw), big),
                  pl.BlockSpec((bsz, ts, bw), big),
                  pl.BlockSpec((bsz, ts, bw), big),
                  pl.BlockSpec((bsz, ts, bw), big),
                  pl.BlockSpec((bsz, ts, nh * CHUNK), big),
                  pl.BlockSpec((bsz, ts // CHUNK, 1, bw), lambda i: (0, i, 0, 0)),
                  pl.BlockSpec((bsz, ts, bw), big),
                  pl.BlockSpec((1, HEAD_DIM), lambda i: (0, 0))],
        out_specs=pl.BlockSpec((bsz, ts, bw), big),
        scratch_shapes=[pltpu.VMEM((bsz * nh, HEAD_DIM, HEAD_DIM), F32)],
        compiler_params=_cparams(1),
        name="gdn_scan",
    )(u0, w, qd, kt, qk, dec, z_all, o_norm_w.reshape(1, HEAD_DIM))


def _conformer_kernel(u_ref, halo_ref, z_ref, cw_ref, cb_ref, lw_ref, lb_ref, y_ref, xx_ref):
    i = pl.program_id(1)
    ts = u_ref.shape[1]
    pad = halo_ref.shape[1]
    xx_ref[0:pad, :] = jnp.where(i > 0, halo_ref[0], 0.0)
    xx_ref[pad:, :] = u_ref[0]
    cw = cw_ref[...]
    off = pad - (CONF_CONV - 1)
    acc = None
    for r in range(8):
        rows = ts if r == 0 else ts + 8
        y = None
        for m in range((pad + 8) // 8):
            k = 8 * m + r - off
            if 0 <= k < CONF_CONV:
                term = cw[k:k + 1, :] * xx_ref[8 * m:8 * m + rows, :]
                y = term if y is None else y + term
        shifted = y[r:r + ts, :]
        acc = shifted if acc is None else acc + shifted
    u = acc + cb_ref[...]
    xc = u - jnp.mean(u, axis=-1, keepdims=True)
    y = xc * lax.rsqrt(jnp.mean(xc * xc, axis=-1, keepdims=True) + NORM_EPS)
    y = _silu(y * lw_ref[...] + lb_ref[...])
    y_ref[0] = (y * z_ref[0].astype(F32)).astype(BF16)


def _conformer(u, z_all, conv_w, conv_b, ln_w, ln_b, ts):
    bsz, seq, bw = u.shape
    pad = 32
    big = lambda b, i: (b, i, 0)
    vec = lambda b, i: (0, 0)
    return pl.pallas_call(
        _conformer_kernel,
        out_shape=jax.ShapeDtypeStruct((bsz, seq, bw), BF16),
        grid=(bsz, seq // ts),
        in_specs=[pl.BlockSpec((1, ts, bw), big),
                  pl.BlockSpec((1, pad, bw), lambda b, i: (b, jnp.maximum(i * (ts // pad) - 1, 0), 0)),
                  pl.BlockSpec((1, ts, bw), lambda b, i: (b, i, 1)),
                  pl.BlockSpec((CONF_CONV, bw), vec),
                  pl.BlockSpec((1, bw), vec),
                  pl.BlockSpec((1, bw), vec),
                  pl.BlockSpec((1, bw), vec)],
        out_specs=pl.BlockSpec((1, ts, bw), big),
        scratch_shapes=[pltpu.VMEM((ts + pad, bw), F32)],
        compiler_params=_cparams(2),
        name="conformer",
    )(u, u, z_all, conv_w, conv_b.reshape(1, bw), ln_w.reshape(1, bw), ln_b.reshape(1, bw))


FOX_SUB = 256
FOX_HEADS = 4
FOX_AHEAD = 4


def _fox_kernel(qi_ref, kj_ref, qt_ref, k_ref, vt_ref, c_ref, z_ref, y_ref, m_ref, l_ref, acc_ref):
    t = pl.program_id(2)
    i = qi_ref[t]
    j = kj_ref[t]
    tq = qt_ref.shape[2]
    tk = k_ref.shape[1]
    hp = c_ref.shape[1]
    nsub = tq // FOX_SUB
    streams = [(g, n) for g in range(hp) for n in range(nsub)]

    @pl.when(j == 0)
    def _():
        m_ref[...] = jnp.full_like(m_ref, -jnp.inf)
        l_ref[...] = jnp.zeros_like(l_ref)
        acc_ref[...] = jnp.zeros_like(acc_ref)

    gate_rows = lax.broadcasted_iota(jnp.int32, (c_ref.shape[3], FOX_SUB), 0)
    minus_gate = jnp.where(gate_rows < N_SPLIT, -1.0, 0.0).astype(BF16)

    def step(diag):
        rows = [min((n + 1) * FOX_SUB, tk) if diag else tk for n in range(nsub)]
        def scores(g, n):
            hd = slice(g * HEAD_DIM, (g + 1) * HEAD_DIM)
            k_aug = jnp.concatenate([k_ref[0, :rows[n], hd], c_ref[0, g, :rows[n], :]], axis=1)
            q_aug = jnp.concatenate([qt_ref[0, hd, n * FOX_SUB:(n + 1) * FOX_SUB], minus_gate], axis=0)
            return _dot(k_aug, q_aug)

        ahead = min(FOX_AHEAD, len(streams))
        s_all = [scores(g, n) for g, n in streams[:ahead]]
        for idx, (g, n) in enumerate(streams):
            s = s_all[idx]
            if idx + ahead < len(streams):
                s_all.append(scores(*streams[idx + ahead]))
            hd = slice(g * HEAD_DIM, (g + 1) * HEAD_DIM)
            cols = slice(n * FOX_SUB, (n + 1) * FOX_SUB)
            if diag:
                kpos = lax.broadcasted_iota(jnp.int32, s.shape, 0)
                qpos = lax.broadcasted_iota(jnp.int32, s.shape, 1) + n * FOX_SUB
                s = jnp.where(kpos <= qpos, s, -jnp.inf)
            m_prev = m_ref[g:g + 1, cols]
            m_new = jnp.maximum(m_prev, jnp.max(s, axis=0, keepdims=True))
            p = jnp.exp2(s - m_new)
            alpha = jnp.exp2(m_prev - m_new)
            l_ref[g:g + 1, cols] = alpha * l_ref[g:g + 1, cols] + jnp.sum(p, axis=0, keepdims=True)
            acc_ref[hd, cols] = alpha * acc_ref[hd, cols] + _dot(vt_ref[0, hd, :rows[n]], p.astype(BF16))
            m_ref[g:g + 1, cols] = m_new

    @pl.when(j < i)
    def _():
        step(False)

    @pl.when(j == i)
    def _():
        step(True)
        for g in range(hp):
            hd = slice(g * HEAD_DIM, (g + 1) * HEAD_DIM)
            o = (acc_ref[hd, :] / l_ref[g:g + 1, :]).T
            y_ref[0, :, hd] = (o * z_ref[0, :, hd].astype(F32)).astype(BF16)


def _fox_attention(q_t, k, v_t, c_split, z_all, t):
    bsz, bw, seq = q_t.shape
    nh = bw // HEAD_DIM
    hp = FOX_HEADS if nh % FOX_HEADS == 0 else 1
    hw = hp * HEAD_DIM
    nt = seq // t
    steps = [(i, j) for i in range(nt) for j in range(i + 1)]
    qi = jnp.asarray([s[0] for s in steps], jnp.int32)
    kj = jnp.asarray([s[1] for s in steps], jnp.int32)
    grid_spec = pltpu.PrefetchScalarGridSpec(
        num_scalar_prefetch=2,
        grid=(bsz, nh // hp, len(steps)),
        in_specs=[pl.BlockSpec((1, hw, t), lambda b, h, s, qi, kj: (b, h, qi[s])),
                  pl.BlockSpec((1, t, hw), lambda b, h, s, qi, kj: (b, kj[s], h)),
                  pl.BlockSpec((1, hw, t), lambda b, h, s, qi, kj: (b, h, kj[s])),
                  pl.BlockSpec((1, hp, t, 128), lambda b, h, s, qi, kj: (b, h, kj[s], 0)),
                  pl.BlockSpec((1, t, hw), lambda b, h, s, qi, kj: (b, qi[s], 2 * (nh // hp) + h))],
        out_specs=pl.BlockSpec((1, t, hw), lambda b, h, s, qi, kj: (b, qi[s], h)),
        scratch_shapes=[pltpu.VMEM((hp, t), F32), pltpu.VMEM((hp, t), F32), pltpu.VMEM((hw, t), F32)],
    )
    return pl.pallas_call(
        _fox_kernel,
        out_shape=jax.ShapeDtypeStruct((bsz, seq, bw), BF16),
        grid_spec=grid_spec,
        compiler_params=_cparams(3),
        name="fox_attention",
    )(qi, kj, q_t, k, v_t, c_split, z_all)


def _merge_kernel(h_ref, ya_ref, yb_ref, yc_ref, wg0_ref, wg1_ref, wg2_ref, wb0_ref, wb1_ref, wb2_ref, o_ref):
    h = h_ref[...]
    acc = None
    for y_ref, wg_ref, wb_ref in ((ya_ref, wg0_ref, wb0_ref), (yb_ref, wg1_ref, wb1_ref), (yc_ref, wg2_ref, wb2_ref)):
        term = _sigmoid(_dot_nt(h, wg_ref[0])) * _dot(y_ref[...], wb_ref[0, 0])
        acc = term if acc is None else acc + term
    o_ref[...] = acc.astype(BF16)


def _merge(h, ya, yb, yc, w_packed, gate_col0, w_branch, layer, tm, tn):
    m, d = h.shape
    bw = ya.shape[1]
    nj = d // tn
    g0 = gate_col0 // tn
    row = lambda j, i: (i, 0)
    return pl.pallas_call(
        _merge_kernel,
        out_shape=jax.ShapeDtypeStruct((m, d), BF16),
        grid=(nj, m // tm),
        in_specs=[pl.BlockSpec((tm, d), row),
                  pl.BlockSpec((tm, bw), row), pl.BlockSpec((tm, bw), row), pl.BlockSpec((tm, bw), row),
                  pl.BlockSpec((1, tn, d), lambda j, i: (layer, g0 + j, 0)),
                  pl.BlockSpec((1, tn, d), lambda j, i: (layer, g0 + nj + j, 0)),
                  pl.BlockSpec((1, tn, d), lambda j, i: (layer, g0 + 2 * nj + j, 0)),
                  pl.BlockSpec((1, 1, bw, tn), lambda j, i: (layer, 0, 0, j)),
                  pl.BlockSpec((1, 1, bw, tn), lambda j, i: (layer, 1, 0, j)),
                  pl.BlockSpec((1, 1, bw, tn), lambda j, i: (layer, 2, 0, j))],
        out_specs=pl.BlockSpec((tm, tn), lambda j, i: (i, j)),
        compiler_params=_cparams(2),
        name="gated_merge",
    )(h, ya, yb, yc, w_packed, w_packed, w_packed, w_branch, w_branch, w_branch)


def _out_kernel(m_ref, w_ref, x_ref, pw_ref, *rest):
    out = _dot(m_ref[...], w_ref[0])
    y = out * lax.rsqrt(jnp.mean(out * out, axis=-1, keepdims=True) + NORM_EPS)
    x_new = x_ref[...] + y * pw_ref[...]
    if len(rest) == 1:
        (o_ref,) = rest
    else:
        nw_ref, o_ref, h_ref = rest
        hn = x_new * lax.rsqrt(jnp.mean(x_new * x_new, axis=-1, keepdims=True) + NORM_EPS)
        h_ref[...] = (hn * nw_ref[...]).astype(BF16)
    o_ref[...] = x_new


def _out_proj(merged, w_out, layer, x2, post_w, next_pre_w, tm):
    m, d = x2.shape
    row = pl.BlockSpec((tm, d), lambda i: (i, 0))
    vec = pl.BlockSpec((1, d), lambda i: (0, 0))
    in_specs = [row, pl.BlockSpec((1, d, d), lambda i: (layer, 0, 0)), row, vec]
    args = [merged, w_out, x2, post_w.reshape(1, d)]
    out_shape, out_specs = jax.ShapeDtypeStruct((m, d), F32), row
    if next_pre_w is not None:
        in_specs.append(vec)
        args.append(next_pre_w.reshape(1, d))
        out_shape, out_specs = (out_shape, jax.ShapeDtypeStruct((m, d), BF16)), (row, row)
    return pl.pallas_call(
        _out_kernel,
        out_shape=out_shape,
        grid=(m // tm,),
        in_specs=in_specs,
        out_specs=out_specs,
        compiler_params=_cparams(1),
        name="out_proj",
    )(*args)


def _layer(x2, h, bsz, seq, l, shared, p, next_pre_w):
    m, d = x2.shape
    bw = d // 2
    tm = min(1024, m)

    wp = shared["w_packed"]
    col = _packed_columns(d)
    qkv_a = _proj(h, wp, l, (col["qkv_a"],), 3 * bw, epilogue="f32", tm=tm, tn=1024, out_dtype=F32)
    z_all = _proj(h, wp, l, (col["z_a"], col["z_b"], col["z_c"]), bw, epilogue="silu_bf16", tm=tm, tn=1024,
                  out_dtype=BF16)
    u_b = _proj_glu(h, wp, l, col["glu"], bw, tm=tm, tn=512)
    q_t = _proj_t(h, wp, l, col["q_c"], bw, bsz=bsz, tm=min(1024, seq), tn=1024,
                  scale=HEAD_DIM ** -0.5 * LOG2_E)
    k_c = _proj(h, wp, l, (col["k_c"],), bw, epilogue="bf16", tm=tm, tn=1024, out_dtype=BF16)
    v_t = _proj_t(h, wp, l, col["v_c"], bw, bsz=bsz, tm=min(1024, seq), tn=1024)
    small_c, small_ch = _small_proj(h, shared["w_small"], shared["prm_c"], l, min(512, seq))

    z3 = z_all.reshape(bsz, seq, 3 * bw)
    u0, w, qd, kt, qk, dec = _gdn_local(qkv_a.reshape(bsz, seq, 3 * bw), p["conv_qkv_w"], small_c, small_ch,
                                        min(256, seq))
    y_a = _gdn_scan(u0, w, qd, kt, qk, dec, z3, p["o_norm_w"], min(512, seq))

    y_b = _conformer(u_b.reshape(bsz, seq, bw), z3, p["conv_w"], p["conv_b"], p["ln_w"], p["ln_b"], min(256, seq))

    c_split = _forget_cumsum(small_c.reshape(bsz, seq, N_SMALL), bw // HEAD_DIM, min(1024, seq))
    y_c = _fox_attention(q_t, k_c.reshape(bsz, seq, bw), v_t, c_split, z3, min(1024, seq))

    merged = _merge(h, y_a.reshape(m, bw), y_b.reshape(m, bw), y_c.reshape(m, bw),
                    wp, col["gate"], shared["w_branch"], l, min(512, m), 512)
    out = _out_proj(merged, shared["w_out"], l, x2, p["post_w"], next_pre_w, min(256, m))
    return out if next_pre_w is not None else (out, None)


REPACK_TR = 512


def _packed_columns(d):
    bw = d // 2
    names = (("qkv_a", 3 * bw), ("z_a", bw), ("glu", 2 * bw), ("z_b", bw), ("q_c", bw), ("k_c", bw),
             ("v_c", bw), ("z_c", bw), ("gate", N_BRANCH * d))
    col, off = {}, 0
    for name, width in names:
        col[name] = off
        off += width
    col["total"] = off
    return col


def _repack_kernel(x_ref, o_ref):
    o_ref[...] = x_ref[...].astype(BF16)


def _repack_w_in(w_t, nh):
    depth, n_in, d = w_t.shape
    col = _packed_columns(d)
    tr = REPACK_TR
    t_glu, t_gate = col["glu"] // tr, col["gate"] // tr
    assert col["glu"] % tr == 0 and col["gate"] % tr == 0 and col["total"] % tr == 0
    assert col["total"] + 3 * nh == n_in

    def src_row(t):
        row = t * tr + jnp.where(t >= t_gate, 3 * nh, jnp.where(t >= t_glu, 2 * nh, 0))
        return pl.multiple_of(row, nh)
    return pl.pallas_call(
        _repack_kernel,
        out_shape=jax.ShapeDtypeStruct((depth, col["total"], d), BF16),
        grid=(depth, col["total"] // tr),
        in_specs=[pl.BlockSpec((pl.Element(1), pl.Element(tr), pl.Element(d)), lambda l, t: (l, src_row(t), 0))],
        out_specs=pl.BlockSpec((1, tr, d), lambda l, t: (l, t, 0)),
        compiler_params=_cparams(2),
        name="repack_w_in",
    )(w_t)


def _small_w_kernel(x_ref, y_ref, o_ref):
    o_ref[...] = jnp.zeros_like(o_ref)
    nx, ny = x_ref.shape[1], y_ref.shape[1]
    o_ref[0, 0:nx, :] = x_ref[0]
    o_ref[0, nx:nx + ny, :] = y_ref[0]


def _small_w(w_t, nh):
    depth, _, d = w_t.shape
    bw = d // 2
    beta0 = 4 * bw
    forget0 = beta0 + 2 * nh + 7 * bw
    return pl.pallas_call(
        _small_w_kernel,
        out_shape=jax.ShapeDtypeStruct((depth, N_SMALL, d), F32),
        grid=(depth,),
        in_specs=[pl.BlockSpec((pl.Element(1), pl.Element(2 * nh), pl.Element(d)), lambda l: (l, beta0, 0)),
                  pl.BlockSpec((pl.Element(1), pl.Element(nh), pl.Element(d)), lambda l: (l, forget0, 0))],
        out_specs=pl.BlockSpec((1, N_SMALL, d), lambda l: (l, 0, 0)),
        compiler_params=_cparams(1),
        name="small_w",
    )(w_t, w_t)


def _prep_params(pre_norm_w, post_norm_w, w_in, conv_qkv_w, a_log, dt_bias, o_norm_w,
                 conv_w, conv_b, ln_w, ln_b, f_bias, w_branch, w_out):
    depth, d, _ = w_in.shape
    bw = d // 2
    nh = bw // HEAD_DIM
    w_t = jnp.swapaxes(w_in, 1, 2)
    w_packed = _repack_w_in(w_t, nh)
    w_small = _small_w(w_t, nh)
    zeros = jnp.zeros((depth, nh), F32)
    ones = jnp.ones((depth, nh), F32)
    vec24 = lambda a, b, c: jnp.concatenate([a, b, c], axis=1)
    prm = jnp.stack([vec24(ones, jnp.exp(a_log.astype(F32)), ones),
                     vec24(zeros, dt_bias.astype(F32), zeros),
                     vec24(zeros, zeros, f_bias.astype(F32))], axis=1)
    prm_c = jnp.pad(prm, ((0, 0), (0, 8 - 3), (0, N_SMALL - 3 * nh)))
    shared = {"w_packed": w_packed, "w_small": w_small, "prm_c": prm_c,
              "w_branch": w_branch.astype(BF16), "w_out": w_out.astype(BF16)}
    per_layer = {"post_w": post_norm_w, "conv_qkv_w": conv_qkv_w, "o_norm_w": o_norm_w,
                 "conv_w": conv_w, "conv_b": conv_b, "ln_w": ln_w, "ln_b": ln_b}
    return shared, per_layer


def kernel(x, pre_norm_w, post_norm_w, w_in, conv_qkv_w, a_log, dt_bias, o_norm_w, conv_w, conv_b, ln_w, ln_b, f_bias, w_branch, w_out):
    bsz, seq, d = x.shape
    depth = w_in.shape[0]
    shared, per_layer = _prep_params(pre_norm_w, post_norm_w, w_in, conv_qkv_w, a_log, dt_bias, o_norm_w,
                                     conv_w, conv_b, ln_w, ln_b, f_bias, w_branch, w_out)
    x2 = x.reshape(bsz * seq, d)
    h = _prenorm(x2, pre_norm_w[0], min(512, bsz * seq))
    for l in range(depth):
        next_pre_w = pre_norm_w[l + 1] if l + 1 < depth else None
        x2, h = _layer(x2, h, bsz, seq, l, shared, {k: v[l] for k, v in per_layer.items()}, next_pre_w)
    return x2.reshape(bsz, seq, d)
```
